```python
import jax, jax.numpy as jnp
from jax import lax
import numpy as np

D_MODEL = 1024
BATCH = 4
SEQ = 8192
DEPTH = 2

CHUNK = 64
Q_BLOCK = 128
EPS = 1e-6
D_FF = 4 * D_MODEL
A_HEADS = 4
A_DQK = 128
A_DV = D_MODEL // A_HEADS
B_HEADS = 8
B_NOPE = 128
B_ROPE = 64
B_VDIM = D_MODEL // B_HEADS
B_QLORA = 384
B_KVLORA = 256
ROPE_THETA = 10000.0
N_A = DEPTH // 2
N_B = DEPTH - N_A
A_PROJ = A_HEADS * (2 * A_DQK + A_DV) + D_MODEL + 2 * A_HEADS

kernel_name = "yoco_mlstm_mla_hybrid"


def rms_norm(x, g):
    xf = x.astype(jnp.float32)
    y = xf * lax.rsqrt(jnp.mean(xf * xf, axis=-1, keepdims=True) + EPS)
    return (y * g.astype(jnp.float32)).astype(x.dtype)


def modulate(h, shift, scale):
    return h * (1 + scale[:, None, :]) + shift[:, None, :]


def rope_tables(positions):
    inv = ROPE_THETA ** (-jnp.arange(0, B_ROPE, 2, dtype=jnp.float32) / B_ROPE)
    ang = positions.astype(jnp.float32)[..., None] * inv
    return jnp.cos(ang), jnp.sin(ang)


def apply_rope(x, cos, sin):
    x1, x2 = jnp.split(x.astype(jnp.float32), 2, axis=-1)
    out = jnp.concatenate([x1 * cos - x2 * sin, x1 * sin + x2 * cos], axis=-1)
    return out.astype(x.dtype)


def sq_relu_mlp(h, w1, w2):
    return jnp.square(jax.nn.relu(h @ w1)) @ w2


def mlstm_chunkwise(q, k, v, i_pre, f_pre):
    Bn, S, H, dk = q.shape
    dv = v.shape[-1]
    nc = S // CHUNK
    f32 = jnp.float32

    def chunks(t):
        return t.astype(f32).reshape(Bn, nc, CHUNK, H, t.shape[-1]).transpose(1, 0, 3, 2, 4)

    def gchunks(t):
        return t.astype(f32).reshape(Bn, nc, CHUNK, H).transpose(1, 0, 3, 2)

    qc = chunks(q) * (dk ** -0.5)
    kc, vc = chunks(k), chunks(v)
    ic = gchunks(i_pre)
    bc = jnp.cumsum(jax.nn.log_sigmoid(gchunks(f_pre)), axis=-1)
    causal = jnp.tril(jnp.ones((CHUNK, CHUNK), dtype=bool))

    def step(carry, xs):
        C, n, m = carry
        qj, kj, vj, ij, bj = xs
        D = bj[..., :, None] - bj[..., None, :] + ij[..., None, :]
        D = jnp.where(causal, D, -jnp.inf)
        inter = bj + m[..., None]
        m_t = jnp.maximum(inter, jnp.max(D, axis=-1))
        w_inter = jnp.exp(inter - m_t)
        P = jnp.exp(D - m_t[..., None]) * jnp.einsum('bhld,bhsd->bhls', qj, kj)
        num = w_inter[..., None] * jnp.einsum('bhld,bhde->bhle', qj, C) + jnp.einsum('bhls,bhse->bhle', P, vj)
        den = w_inter * jnp.einsum('bhld,bhd->bhl', qj, n) + jnp.sum(P, axis=-1)
        h = num / jnp.maximum(jnp.abs(den), jnp.exp(-m_t))[..., None]
        b_last = bj[..., -1]
        g = b_last[..., None] - bj + ij
        m_new = jnp.maximum(b_last + m, jnp.max(g, axis=-1))
        decay = jnp.exp(b_last + m - m_new)
        wk = jnp.exp(g - m_new[..., None])
        C = decay[..., None, None] * C + jnp.einsum('bhs,bhsd,bhse->bhde', wk, kj, vj)
        n = decay[..., None] * n + jnp.einsum('bhs,bhsd->bhd', wk, kj)
        return (C, n, m_new), h

    init = (jnp.zeros((Bn, H, dk, dv), f32), jnp.zeros((Bn, H, dk), f32), jnp.zeros((Bn, H), f32))
    _, hs = lax.scan(step, init, (qc, kc, vc, ic, bc))
    return hs.transpose(1, 0, 3, 2, 4).reshape(Bn, S, H, dv).astype(q.dtype)


def mlstm_mixer(h, w_in, b_gates, head_norm, w_out):
    Bn, S, _ = h.shape
    sizes = [A_HEADS * A_DQK, A_HEADS * A_DQK, A_HEADS * A_DV, D_MODEL, A_HEADS, A_HEADS]
    idx = [int(s) for s in np.cumsum(sizes)[:-1]]
    q, k, v, o_pre, i_pre, f_pre = jnp.split(h @ w_in, idx, axis=-1)
    q = q.reshape(Bn, S, A_HEADS, A_DQK)
    k = k.reshape(Bn, S, A_HEADS, A_DQK)
    v = v.reshape(Bn, S, A_HEADS, A_DV)
    i_pre = i_pre + b_gates[:A_HEADS]
    f_pre = f_pre + b_gates[A_HEADS:]
    hh = mlstm_chunkwise(q, k, v, i_pre, f_pre)
    hh = rms_norm(hh, head_norm).reshape(Bn, S, D_MODEL)
    return (jax.nn.sigmoid(o_pre) * hh) @ w_out


def mla_shared_kv(x, c_act, kv_mod_w, kv_mod_b, kv_norm, w_dkv, kv_lora_norm, w_ukv, cos, sin):
    Bn, S, _ = x.shape
    shift, scale = jnp.split(c_act @ kv_mod_w + kv_mod_b, 2, axis=-1)
    hs = modulate(rms_norm(x, kv_norm), shift, scale)
    ckv, k_rope = jnp.split(hs @ w_dkv, [B_KVLORA], axis=-1)
    ckv = rms_norm(ckv, kv_lora_norm)
    kv = (ckv @ w_ukv).reshape(Bn, S, B_HEADS, B_NOPE + B_VDIM)
    k_nope, v = jnp.split(kv, [B_NOPE], axis=-1)
    k_rope = apply_rope(k_rope, cos, sin)
    return k_nope, k_rope, v


def mla_attention(q_nope, q_rope, k_nope, k_rope, v):
    Bn, S = q_nope.shape[:2]
    nb = S // Q_BLOCK
    scale = (B_NOPE + B_ROPE) ** -0.5
    key_chunk = jnp.arange(S) // CHUNK
    qn_b = q_nope.reshape(Bn, nb, Q_BLOCK, B_HEADS, B_NOPE).transpose(1, 0, 2, 3, 4)
    qr_b = q_rope.reshape(Bn, nb, Q_BLOCK, B_HEADS, B_ROPE).transpose(1, 0, 2, 3, 4)

    def block(args):
        qn, qr, blk = args
        s = jnp.einsum('bqhd,bkhd->bhqk', qn, k_nope) + jnp.einsum('bqhd,bkd->bhqk', qr, k_rope)
        s = s.astype(jnp.float32) * scale
        q_chunk = (blk * Q_BLOCK + jnp.arange(Q_BLOCK)) // CHUNK
        mask = key_chunk[None, :] <= q_chunk[:, None]
        p = jax.nn.softmax(jnp.where(mask, s, -jnp.inf), axis=-1).astype(v.dtype)
        return jnp.einsum('bhqk,bkhd->bqhd', p, v)

    out = lax.map(block, (qn_b, qr_b, jnp.arange(nb)))
    return out.transpose(1, 0, 2, 3, 4).reshape(Bn, S, B_HEADS * B_VDIM)


def mla_mixer(h, w_qa, q_norm, w_qb, w_o, k_nope, k_rope, v, cos, sin):
    Bn, S, _ = h.shape
    q = (rms_norm(h @ w_qa, q_norm) @ w_qb).reshape(Bn, S, B_HEADS, B_NOPE + B_ROPE)
    q_nope, q_rope = jnp.split(q, [B_NOPE], axis=-1)
    q_rope = apply_rope(q_rope, cos[:, :, None, :], sin[:, :, None, :])
    return mla_attention(q_nope, q_rope, k_nope, k_rope, v) @ w_o


def setup_inputs(seed: int = 0) -> dict:
    key = jax.random.key(seed)
    ks = jax.random.split(key, 24)
    nrm = lambda k, shape, fan: jax.random.normal(k, shape, jnp.float32) * fan ** -0.5
    x = jax.random.normal(ks[0], (BATCH, SEQ, D_MODEL), jnp.float32)
    c = jax.random.normal(ks[1], (BATCH, D_MODEL), jnp.float32)
    offsets = jax.random.randint(ks[2], (BATCH, 1), 0, 64) * CHUNK
    positions = (offsets + jnp.arange(SEQ, dtype=jnp.int32)[None, :]).astype(jnp.int32)
    mod_w = nrm(ks[3], (DEPTH, D_MODEL, 6 * D_MODEL), D_MODEL) * 0.5
    mod_b = 0.01 * jax.random.normal(ks[4], (DEPTH, 6 * D_MODEL), jnp.float32)
    norm_g = 1.0 + 0.02 * jax.random.normal(ks[5], (DEPTH, 4, D_MODEL), jnp.float32)
    ffn_w1 = nrm(ks[6], (DEPTH, D_MODEL, D_FF), D_MODEL)
    ffn_w2 = nrm(ks[7], (DEPTH, D_FF, D_MODEL), D_FF)
    a_w_in = nrm(ks[8], (N_A, D_MODEL, A_PROJ), D_MODEL)
    i_bias = 0.1 * jax.random.normal(ks[9], (N_A, A_HEADS), jnp.float32)
    f_bias = jnp.linspace(3.0, 6.0, A_HEADS, dtype=jnp.float32)[None, :] + 0.1 * jax.random.normal(ks[10], (N_A, A_HEADS), jnp.float32)
    a_b_gates = jnp.concatenate([i_bias, f_bias], axis=-1)
    a_head_norm = 1.0 + 0.02 * jax.random.normal(ks[11], (N_A, A_HEADS, A_DV), jnp.float32)
    a_w_out = nrm(ks[12], (N_A, D_MODEL, D_MODEL), D_MODEL)
    b_w_qa = nrm(ks[13], (N_B, D_MODEL, B_QLORA), D_MODEL)
    b_q_norm = 1.0 + 0.02 * jax.random.normal(ks[14], (N_B, B_QLORA), jnp.float32)
    b_w_qb = nrm(ks[15], (N_B, B_QLORA, B_HEADS * (B_NOPE + B_ROPE)), B_QLORA)
    b_w_o = nrm(ks[16], (N_B, B_HEADS * B_VDIM, D_MODEL), B_HEADS * B_VDIM)
    kv_mod_w = nrm(ks[17], (D_MODEL, 2 * D_MODEL), D_MODEL) * 0.5
    kv_mod_b = 0.01 * jax.random.normal(ks[18], (2 * D_MODEL,), jnp.float32)
    kv_norm = 1.0 + 0.02 * jax.random.normal(ks[19], (D_MODEL,), jnp.float32)
    w_dkv = nrm(ks[20], (D_MODEL, B_KVLORA + B_ROPE), D_MODEL)
    kv_lora_norm = 1.0 + 0.02 * jax.random.normal(ks[21], (B_KVLORA,), jnp.float32)
    w_ukv = nrm(ks[22], (B_KVLORA, B_HEADS * (B_NOPE + B_VDIM)), B_KVLORA)
    return {"x": x, "c": c, "positions": positions, "mod_w": mod_w, "mod_b": mod_b, "norm_g": norm_g,
            "ffn_w1": ffn_w1, "ffn_w2": ffn_w2, "a_w_in": a_w_in, "a_b_gates": a_b_gates,
            "a_head_norm": a_head_norm, "a_w_out": a_w_out, "b_w_qa": b_w_qa, "b_q_norm": b_q_norm,
            "b_w_qb": b_w_qb, "b_w_o": b_w_o, "kv_mod_w": kv_mod_w, "kv_mod_b": kv_mod_b,
            "kv_norm": kv_norm, "w_dkv": w_dkv, "kv_lora_norm": kv_lora_norm, "w_ukv": w_ukv}


def reference(x, c, positions, mod_w, mod_b, norm_g, ffn_w1, ffn_w2, a_w_in, a_b_gates, a_head_norm,
              a_w_out, b_w_qa, b_q_norm, b_w_qb, b_w_o, kv_mod_w, kv_mod_b, kv_norm, w_dkv,
              kv_lora_norm, w_ukv):
    cos, sin = rope_tables(positions)
    c_act = jax.nn.silu(c)
    k_nope = k_rope = v = None
    for layer in range(DEPTH):
        sh1, sc1, g1, sh2, sc2, g2 = jnp.split(c_act @ mod_w[layer] + mod_b[layer], 6, axis=-1)
        h = modulate(rms_norm(x, norm_g[layer, 0]), sh1, sc1)
        if layer < N_A:
            y = mlstm_mixer(h, a_w_in[layer], a_b_gates[layer], a_head_norm[layer], a_w_out[layer])
        else:
            if layer == N_A:
                k_nope, k_rope, v = mla_shared_kv(x, c_act, kv_mod_w, kv_mod_b, kv_norm, w_dkv,
                                                  kv_lora_norm, w_ukv, cos, sin)
            j = layer - N_A
            y = mla_mixer(h, b_w_qa[j], b_q_norm[j], b_w_qb[j], b_w_o[j], k_nope, k_rope, v, cos, sin)
        x = x + g1[:, None, :] * rms_norm(y, norm_g[layer, 1])
        h = modulate(rms_norm(x, norm_g[layer, 2]), sh2, sc2)
        y = sq_relu_mlp(h, ffn_w1[layer], ffn_w2[layer])
        x = x + g2[:, None, :] * rms_norm(y, norm_g[layer, 3])
    return x
```

```python
import functools

import jax
import jax.numpy as jnp
from jax import lax
from jax.experimental import pallas as pl
from jax.experimental.pallas import tpu as pltpu

D_MODEL = 1024
CHUNK = 64
EPS = 1e-6
D_FF = 4 * D_MODEL
A_HEADS = 4
A_DQK = 128
A_DV = D_MODEL // A_HEADS
B_HEADS = 8
B_NOPE = 128
B_ROPE = 64
B_VDIM = D_MODEL // B_HEADS
B_QLORA = 384
B_KVLORA = 256
ROPE_THETA = 10000.0

LANES = 128
QK_PAD = 256
VMEM_LIMIT = 56 * 1024 * 1024

F32 = jnp.float32
BF16 = jnp.bfloat16


def _params(sem, vmem=VMEM_LIMIT):
    return pltpu.CompilerParams(dimension_semantics=sem, vmem_limit_bytes=vmem)


def _const_spec(shape):
    nd = len(shape)
    return pl.BlockSpec(shape, lambda *_: (0,) * nd, pipeline_mode=pl.Buffered(1))


def _rms(x):
    return x * lax.rsqrt(jnp.mean(x * x, axis=-1, keepdims=True) + EPS)


def _split3(x):
    hi = x.astype(BF16)
    r = x - hi.astype(F32)
    mid = r.astype(BF16)
    lo = (r - mid.astype(F32)).astype(BF16)
    return hi, mid, lo


def _mod_kernel(c_ref, w_ref, b_ref, o_ref):
    c = c_ref[...]
    c_act = c * (1.0 / (1.0 + jnp.exp(-c)))
    w = w_ref[0].astype(BF16)
    o_ref[0] = jnp.dot(c_act.astype(BF16), w, preferred_element_type=F32) + b_ref[0]


def _mod_vectors(c8, w, b, tn=1024):
    g, d, n = w.shape
    return pl.pallas_call(
        _mod_kernel,
        grid=(g, n // tn),
        in_specs=[pl.BlockSpec((8, d), lambda i, j: (0, 0)),
                  pl.BlockSpec((1, d, tn), lambda i, j: (i, 0, j)),
                  pl.BlockSpec((1, 1, tn), lambda i, j: (i, 0, j))],
        out_specs=pl.BlockSpec((1, 8, tn), lambda i, j: (i, 0, j)),
        out_shape=jax.ShapeDtypeStruct((g, 8, n), F32),
        compiler_params=_params(("parallel", "parallel")),
        name="mod_vectors",
    )(c8, w, b)


def _rope_kernel(pos_ref, inv_ref, cos_ref, sin_ref):
    ang = pos_ref[...].astype(F32) * inv_ref[...]
    cos_ref[...] = jnp.cos(ang)
    sin_ref[...] = jnp.sin(ang)


def _rope_tables(positions):
    half = B_ROPE // 2
    t = positions.size
    rows = t * half // LANES
    per_row = LANES // half
    pos = jnp.repeat(positions.reshape(rows, per_row), half, axis=1)
    inv = ROPE_THETA ** (-jnp.arange(0, B_ROPE, 2, dtype=F32) / B_ROPE)
    inv = jnp.tile(inv, per_row).reshape(1, LANES)
    tr = 1024
    cos, sin = pl.pallas_call(
        _rope_kernel,
        grid=(rows // tr,),
        in_specs=[pl.BlockSpec((tr, LANES), lambda i: (i, 0)),
                  pl.BlockSpec((1, LANES), lambda i: (0, 0))],
        out_specs=[pl.BlockSpec((tr, LANES), lambda i: (i, 0))] * 2,
        out_shape=[jax.ShapeDtypeStruct((rows, LANES), F32)] * 2,
        compiler_params=_params(("parallel",)),
        name="rope_tables",
    )(pos, inv)
    return cos.reshape(t, half), sin.reshape(t, half)


def _log_sigmoid(x):
    return jnp.minimum(x, 0.0) - jnp.log1p(jnp.exp(-jnp.abs(x)))


def _inproj_kernel(x_ref, mod_ref, ng_ref, wqvo_ref, wkt_ref, wg_ref, wgt_ref, bcol_ref, brow_ref,
                   q_ref, kt_ref, v_ref, og_ref, gc_ref, gr_ref):
    m = mod_ref[0]
    xhat = _rms(x_ref[0])
    h = (xhat * (ng_ref[0:1] * (1.0 + m[1:2])) + m[0:1]).astype(BF16)
    nq = A_HEADS * A_DQK
    q = jnp.dot(h, wqvo_ref[:, :nq], preferred_element_type=F32)
    q_ref[0] = (q * (A_DQK ** -0.5)).astype(BF16)
    v = jnp.dot(h, wqvo_ref[:, nq:nq + D_MODEL], preferred_element_type=F32)
    v_ref[0] = v.astype(BF16)
    o = jnp.dot(h, wqvo_ref[:, nq + D_MODEL:], preferred_element_type=F32)
    og_ref[0] = (1.0 / (1.0 + jnp.exp(-o))).astype(BF16)
    nt = (((1,), (1,)), ((), ()))
    kt_ref[0] = lax.dot_general(wkt_ref[...], h, nt, preferred_element_type=F32).astype(BF16)
    gc = jnp.dot(h, wg_ref[...], preferred_element_type=F32) + bcol_ref[...]
    col = lax.broadcasted_iota(jnp.int32, gc.shape, 1)
    gc_ref[0] = jnp.where(col < A_HEADS, gc, _log_sigmoid(gc))
    gr = lax.dot_general(wgt_ref[...], h, nt, preferred_element_type=F32) + brow_ref[...]
    row = lax.broadcasted_iota(jnp.int32, gr.shape, 0)
    gr_ref[0] = jnp.where(row < A_HEADS, gr, _log_sigmoid(gr))


def _inproj(x, modp, ng, wqvo, wkt, wg, wgt, bcol, brow, tm=512):
    b, s, d = x.shape
    nq = A_HEADS * A_DQK
    ng8 = 2 * A_HEADS
    tok = lambda w: pl.BlockSpec((1, tm, w), lambda i, j: (i, j, 0))
    return pl.pallas_call(
        _inproj_kernel,
        grid=(b, s // tm),
        in_specs=[tok(d),
                  pl.BlockSpec((1, 8, d), lambda i, j: (i, 0, 0)),
                  _const_spec(ng.shape), _const_spec(wqvo.shape), _const_spec(wkt.shape),
                  _const_spec(wg.shape), _const_spec(wgt.shape), _const_spec(bcol.shape),
                  _const_spec(brow.shape)],
        out_specs=[tok(nq),
                   pl.BlockSpec((1, nq, tm), lambda i, j: (i, 0, j)),
                   tok(d), tok(d), tok(LANES),
                   pl.BlockSpec((1, ng8, tm), lambda i, j: (i, 0, j))],
        out_shape=[jax.ShapeDtypeStruct((b, s, nq), BF16),
                   jax.ShapeDtypeStruct((b, nq, s), BF16),
                   jax.ShapeDtypeStruct((b, s, d), BF16),
                   jax.ShapeDtypeStruct((b, s, d), BF16),
                   jax.ShapeDtypeStruct((b, s, LANES), F32),
                   jax.ShapeDtypeStruct((b, ng8, s), F32)],
        compiler_params=_params(("parallel", "parallel")),
        name="mlstm_inproj",
    )(x, modp, ng, wqvo, wkt, wg, wgt, bcol, brow)


def _mlstm_kernel(q_ref, kt_ref, v_ref, og_ref, gc_ref, gr_ref, hn_ref, o_ref, c_scr, n_scr, m_scr, *, L):
    @pl.when(pl.program_id(1) == 0)
    def _():
        c_scr[...] = jnp.zeros_like(c_scr)
        n_scr[...] = jnp.zeros_like(n_scr)
        m_scr[...] = jnp.zeros_like(m_scr)

    r = lax.broadcasted_iota(jnp.int32, (L, L), 0)
    cc = lax.broadcasted_iota(jnp.int32, (L, L), 1)
    causal = cc <= r
    tril = jnp.where(causal, 1.0, 0.0).astype(BF16)
    triu = jnp.where(r <= cc, 1.0, 0.0).astype(BF16)
    gc = gc_ref[0]
    gr = gr_ref[0]
    bcol_all = sum(jnp.dot(tril, t, preferred_element_type=F32) for t in _split3(gc))
    brow_all = sum(jnp.dot(t, triu, preferred_element_type=F32) for t in _split3(gr))
    nt = (((1,), (1,)), ((), ()))

    for hd in range(A_HEADS):
        fq = hd + A_HEADS
        i_row = gr[hd:hd + 1, :]
        b_row = brow_all[fq:fq + 1, :]
        i_col = gc[:, hd:hd + 1]
        b_col = bcol_all[:, fq:fq + 1]
        m_prev = m_scr[hd][0:1, 0:1]
        q = q_ref[0, :, hd * A_DQK:(hd + 1) * A_DQK]
        kt = kt_ref[0, hd * A_DQK:(hd + 1) * A_DQK, :]
        v = v_ref[0, :, hd * A_DV:(hd + 1) * A_DV]
        c_prev = c_scr[hd]
        n_prev = n_scr[hd][0:1, :]

        dmat = jnp.where(causal, b_col - b_row + i_row, -jnp.inf)
        inter = b_col + m_prev
        m_t = jnp.maximum(inter, jnp.max(dmat, axis=1, keepdims=True))
        w_inter = jnp.exp(inter - m_t)
        p = jnp.exp(dmat - m_t) * jnp.dot(q, kt, preferred_element_type=F32)
        num = (w_inter * jnp.dot(q, c_prev.astype(BF16), preferred_element_type=F32)
               + jnp.dot(p.astype(BF16), v, preferred_element_type=F32))
        den = (w_inter * jnp.sum(q.astype(F32) * n_prev, axis=1, keepdims=True)
               + jnp.sum(p, axis=1, keepdims=True))
        hh = num / jnp.maximum(jnp.abs(den), jnp.exp(-m_t))
        hh = _rms(hh) * hn_ref[hd:hd + 1, :]
        og = og_ref[0, :, hd * A_DV:(hd + 1) * A_DV].astype(F32)
        o_ref[0, :, hd * A_DV:(hd + 1) * A_DV] = (og * hh).astype(o_ref.dtype)

        b_last = b_row[:, L - 1:L]
        g_row = b_last - b_row + i_row
        m_new = jnp.maximum(b_last + m_prev, jnp.max(g_row, axis=1, keepdims=True))
        decay = jnp.exp(b_last + m_prev - m_new)
        wk = jnp.exp(g_row - m_new)
        kw = (kt.astype(F32) * wk).astype(BF16)
        c_scr[hd] = decay * c_prev + jnp.dot(kw, v, preferred_element_type=F32)
        wk8 = jnp.broadcast_to(wk, (8, L))
        n_add = sum(lax.dot_general(t, kt, nt, preferred_element_type=F32) for t in _split3(wk8))
        n_scr[hd] = decay * n_scr[hd] + n_add
        m_scr[hd] = jnp.broadcast_to(m_new, (8, LANES))


def _mlstm(q, kt, v, og, gc, gr, head_norm, L=128):
    b, s, d = v.shape
    nq = A_HEADS * A_DQK
    ng8 = 2 * A_HEADS
    tok = lambda w: pl.BlockSpec((1, L, w), lambda i, j: (i, j, 0))
    return pl.pallas_call(
        functools.partial(_mlstm_kernel, L=L),
        grid=(b, s // L),
        in_specs=[tok(nq),
                  pl.BlockSpec((1, nq, L), lambda i, j: (i, 0, j)),
                  tok(d), tok(d), tok(LANES),
                  pl.BlockSpec((1, ng8, L), lambda i, j: (i, 0, j)),
                  pl.BlockSpec(head_norm.shape, lambda i, j: (0, 0))],
        out_specs=tok(d),
        out_shape=jax.ShapeDtypeStruct((b, s, d), BF16),
        scratch_shapes=[pltpu.VMEM((A_HEADS, A_DQK, A_DV), F32),
                        pltpu.VMEM((A_HEADS, 8, A_DQK), F32),
                        pltpu.VMEM((A_HEADS, 8, LANES), F32)],
        compiler_params=_params(("parallel", "arbitrary")),
        name="mlstm_chunkwise",
    )(q, kt, v, og, gc, gr, head_norm)


def _post_kernel(a_ref, x_ref, mod_ref, ng_ref, wp_ref, w1_ref, w2_ref, o_ref, *, ff_chunk):
    m = mod_ref[0]
    y = jnp.dot(a_ref[0], wp_ref[...], preferred_element_type=F32)
    x1 = x_ref[0] + (m[2:3] * ng_ref[1:2]) * _rms(y)
    h = (_rms(x1) * (ng_ref[2:3] * (1.0 + m[4:5])) + m[3:4]).astype(BF16)
    acc = jnp.zeros_like(x1)
    for c in range(D_FF // ff_chunk):
        u = jnp.dot(h, w1_ref[:, c * ff_chunk:(c + 1) * ff_chunk], preferred_element_type=F32)
        u = jnp.square(jnp.maximum(u, 0.0)).astype(BF16)
        acc = acc + jnp.dot(u, w2_ref[c * ff_chunk:(c + 1) * ff_chunk, :], preferred_element_type=F32)
    o_ref[0] = x1 + (m[5:6] * ng_ref[3:4]) * _rms(acc)


def _post(a, x, modp, ng, wp, w1, w2, tm=512, ff_chunk=1024):
    b, s, d = x.shape
    tok = pl.BlockSpec((1, tm, d), lambda i, j: (i, j, 0))
    return pl.pallas_call(
        functools.partial(_post_kernel, ff_chunk=ff_chunk),
        grid=(b, s // tm),
        in_specs=[tok, tok,
                  pl.BlockSpec((1, 8, d), lambda i, j: (i, 0, 0)),
                  _const_spec(ng.shape), _const_spec(wp.shape), _const_spec(w1.shape),
                  _const_spec(w2.shape)],
        out_specs=tok,
        out_shape=jax.ShapeDtypeStruct((b, s, d), F32),
        compiler_params=_params(("parallel", "parallel")),
        name="outproj_mlp",
    )(a, x, modp, ng, wp, w1, w2)


def _rope_block(blk, cosp, sinp):
    return blk * cosp + pltpu.roll(blk, LANES // 2, axis=1) * sinp


def _qkv_kernel(x_ref, mod_ref, kvmod_ref, ng_ref, kvn_ref, wqa_ref, qn_ref, wqb_ref, wdkv_ref,
                kln_ref, wukv_ref, cos_ref, sin_ref, q_ref, k_ref, v_ref):
    m = mod_ref[0]
    km = kvmod_ref[0]
    xhat = _rms(x_ref[0])
    cosp = cos_ref[0]
    sinp = sin_ref[0]
    scale = (B_NOPE + B_ROPE) ** -0.5

    h = (xhat * (ng_ref[0:1] * (1.0 + m[1:2])) + m[0:1]).astype(BF16)
    qa = jnp.dot(h, wqa_ref[...], preferred_element_type=F32)
    qn = (_rms(qa) * qn_ref[...]).astype(BF16)
    q = jnp.dot(qn, wqb_ref[...], preferred_element_type=F32) * scale
    for hd in range(B_HEADS):
        lo = hd * QK_PAD
        q_ref[0, :, lo:lo + B_NOPE] = q[:, lo:lo + B_NOPE].astype(BF16)
        q_ref[0, :, lo + B_NOPE:lo + QK_PAD] = _rope_block(
            q[:, lo + B_NOPE:lo + QK_PAD], cosp, sinp).astype(BF16)

    hs = (xhat * (kvn_ref[...] * (1.0 + km[1:2])) + km[0:1]).astype(BF16)
    dkv = jnp.dot(hs, wdkv_ref[...], preferred_element_type=F32)
    kr = _rope_block(dkv[:, B_KVLORA:], cosp, sinp).astype(BF16)
    ckv = (_rms(dkv[:, :B_KVLORA]) * kln_ref[...]).astype(BF16)
    kv = jnp.dot(ckv, wukv_ref[...], preferred_element_type=F32)
    for hd in range(B_HEADS):
        lo = hd * QK_PAD
        k_ref[0, :, lo:lo + B_NOPE] = kv[:, hd * B_NOPE:(hd + 1) * B_NOPE].astype(BF16)
        k_ref[0, :, lo + B_NOPE:lo + QK_PAD] = kr
    v_ref[0] = kv[:, B_HEADS * B_NOPE:].astype(BF16)


def _qkv(x, modp, kvmodp, ng, kvn, wqa, qnorm, wqb, wdkv, kln, wukv, cosp, sinp, tm=512):
    b, s, d = x.shape
    tok = lambda w: pl.BlockSpec((1, tm, w), lambda i, j: (i, j, 0))
    modspec = pl.BlockSpec((1, 8, d), lambda i, j: (i, 0, 0))
    hq = B_HEADS * QK_PAD
    return pl.pallas_call(
        _qkv_kernel,
        grid=(b, s // tm),
        in_specs=[tok(d), modspec, modspec,
                  _const_spec(ng.shape), _const_spec(kvn.shape), _const_spec(wqa.shape),
                  _const_spec(qnorm.shape), _const_spec(wqb.shape), _const_spec(wdkv.shape),
                  _const_spec(kln.shape), _const_spec(wukv.shape),
                  tok(LANES), tok(LANES)],
        out_specs=[tok(hq), tok(hq), tok(d)],
        out_shape=[jax.ShapeDtypeStruct((b, s, hq), BF16),
                   jax.ShapeDtypeStruct((b, s, hq), BF16),
                   jax.ShapeDtypeStruct((b, s, d), BF16)],
        compiler_params=_params(("parallel", "parallel")),
        name="mla_qkv",
    )(x, modp, kvmodp, ng, kvn, wqa, qnorm, wqb, wdkv, kln, wukv, cosp, sinp)


def _attn_kernel(q_ref, k_ref, v_ref, o_ref, *, tq, tk):
    i = pl.program_id(2)
    q = q_ref[0]
    nt = (((1,), (1,)), ((), ()))

    def tile(j, carry, masked):
        m, l, acc = carry
        off = pl.multiple_of(j * tk, tk)
        k = k_ref[0, pl.ds(off, tk), :]
        v = v_ref[0, pl.ds(off, tk), :]
        s = lax.dot_general(q, k, nt, preferred_element_type=F32)
        if masked:
            qc = (lax.broadcasted_iota(jnp.int32, (tq, tk), 0) + i * tq) // CHUNK
            kc = (lax.broadcasted_iota(jnp.int32, (tq, tk), 1) + j * tk) // CHUNK
            s = jnp.where(kc <= qc, s, -jnp.inf)
        m_new = jnp.maximum(m, jnp.max(s, axis=1, keepdims=True))
        alpha = jnp.exp(m - m_new)
        p = jnp.exp(s - m_new)
        l = alpha * l + jnp.sum(p, axis=1, keepdims=True)
        acc = alpha * acc + jnp.dot(p.astype(BF16), v, preferred_element_type=F32)
        return m_new, l, acc

    init = (jnp.full((tq, 1), -jnp.inf, F32), jnp.zeros((tq, 1), F32), jnp.zeros((tq, B_VDIM), F32))
    per_q = tq // tk
    carry = lax.fori_loop(0, i * per_q, lambda j, c: tile(j, c, False), init)
    for d in range(per_q):
        carry = tile(i * per_q + d, carry, True)
    _, l, acc = carry
    o_ref[0] = (acc / l).astype(o_ref.dtype)


def _attention(q, k, v, tq=512, tk=512):
    b, s, _ = q.shape
    return pl.pallas_call(
        functools.partial(_attn_kernel, tq=tq, tk=tk),
        grid=(b, B_HEADS, s // tq),
        in_specs=[pl.BlockSpec((1, tq, QK_PAD), lambda bi, h, i: (bi, i, h)),
                  pl.BlockSpec((1, s, QK_PAD), lambda bi, h, i: (bi, 0, h)),
                  pl.BlockSpec((1, s, B_VDIM), lambda bi, h, i: (bi, 0, h))],
        out_specs=pl.BlockSpec((1, tq, B_VDIM), lambda bi, h, i: (bi, i, h)),
        out_shape=jax.ShapeDtypeStruct((b, s, B_HEADS * B_VDIM), BF16),
        compiler_params=_params(("parallel", "parallel", "arbitrary")),
        name="mla_attention",
    )(q, k, v)


def _pack_mod(vec6, b):
    m = vec6[:b].reshape(b, 6, D_MODEL)
    return jnp.pad(m, ((0, 0), (0, 2), (0, 0)))


def _rope_cols(w, n_lead):
    half = B_ROPE // 2
    z = jnp.zeros((w.shape[0], half), w.dtype)
    return jnp.concatenate([w[:, :n_lead], w[:, n_lead:n_lead + half], z, w[:, n_lead + half:], z], axis=1)


def kernel(x, c, positions, mod_w, mod_b, norm_g, ffn_w1, ffn_w2, a_w_in, a_b_gates, a_head_norm,
           a_w_out, b_w_qa, b_q_norm, b_w_qb, b_w_o, kv_mod_w, kv_mod_b, kv_norm, w_dkv,
           kv_lora_norm, w_ukv):
    b, s, d = x.shape

    c8 = jnp.pad(c, ((0, 8 - b), (0, 0)))
    mods = _mod_vectors(c8, mod_w, mod_b[:, None, :])
    kvmod = _mod_vectors(c8, kv_mod_w[None], kv_mod_b[None, None, :])[0]
    modp0 = _pack_mod(mods[0], b)
    modp1 = _pack_mod(mods[1], b)
    kvmodp = jnp.pad(kvmod[:b].reshape(b, 2, d), ((0, 0), (0, 6), (0, 0)))

    w_in = a_w_in[0]
    nq = A_HEADS * A_DQK
    wq, wk, wv, wo, wgate = (w_in[:, :nq], w_in[:, nq:2 * nq], w_in[:, 2 * nq:2 * nq + d],
                             w_in[:, 2 * nq + d:2 * nq + 2 * d], w_in[:, 2 * nq + 2 * d:])
    wqvo = jnp.concatenate([wq, wv, wo], axis=1).astype(BF16)
    wkt = wk.T.astype(BF16)
    ng8 = 2 * A_HEADS
    wg = jnp.pad(wgate, ((0, 0), (0, LANES - ng8))).astype(BF16)
    wgt = wgate.T.astype(BF16)
    bcol = jnp.pad(a_b_gates[0], (0, LANES - ng8)).reshape(1, LANES)
    brow = a_b_gates[0].reshape(ng8, 1)
    q, kt, v, og, gc, gr = _inproj(x, modp0, norm_g[0], wqvo, wkt, wg, wgt, bcol, brow)
    a = _mlstm(q, kt, v, og, gc, gr, a_head_norm[0])
    x = _post(a, x, modp0, norm_g[0], a_w_out[0].astype(BF16), ffn_w1[0].astype(BF16),
              ffn_w2[0].astype(BF16))

    cos, sin = _rope_tables(positions)
    cosp = jnp.tile(cos, (1, 4)).reshape(b, s, LANES)
    sinp = jnp.concatenate([-sin, -sin, sin, sin], axis=1).reshape(b, s, LANES)
    wqb = b_w_qb[0].reshape(B_QLORA * B_HEADS, B_NOPE + B_ROPE)
    wqb = _rope_cols(wqb, B_NOPE).reshape(B_QLORA, B_HEADS * QK_PAD).astype(BF16)
    wdkv = _rope_cols(w_dkv, B_KVLORA).astype(BF16)
    wukv = w_ukv.reshape(B_KVLORA, B_HEADS, B_NOPE + B_VDIM)
    wukv = jnp.concatenate([wukv[:, :, :B_NOPE].reshape(B_KVLORA, -1),
                            wukv[:, :, B_NOPE:].reshape(B_KVLORA, -1)], axis=1).astype(BF16)
    qh, kh, vh = _qkv(x, modp1, kvmodp, norm_g[1], kv_norm.reshape(1, d), b_w_qa[0].astype(BF16),
                      b_q_norm[0].reshape(1, -1), wqb, wdkv, kv_lora_norm.reshape(1, -1), wukv,
                      cosp, sinp)
    a = _attention(qh, kh, vh)
    x = _post(a, x, modp1, norm_g[1], b_w_o[0].astype(BF16), ffn_w1[1].astype(BF16),
              ffn_w2[1].astype(BF16))
    return x
```

```python
import functools

import jax
import jax.numpy as jnp
from jax import lax
from jax.experimental import pallas as pl
from jax.experimental.pallas import tpu as pltpu

D_MODEL = 1024
CHUNK = 64
EPS = 1e-6
D_FF = 4 * D_MODEL
A_HEADS = 4
A_DQK = 128
A_DV = D_MODEL // A_HEADS
B_HEADS = 8
B_NOPE = 128
B_ROPE = 64
B_VDIM = D_MODEL // B_HEADS
B_QLORA = 384
B_KVLORA = 256
ROPE_THETA = 10000.0

LANES = 128
QK_PAD = 256
VMEM_LIMIT = 56 * 1024 * 1024
LOG2E = 1.4426950408889634

F32 = jnp.float32
BF16 = jnp.bfloat16


def _params(sem, vmem=VMEM_LIMIT):
    return pltpu.CompilerParams(dimension_semantics=sem, vmem_limit_bytes=vmem)


def _const_spec(shape):
    nd = len(shape)
    return pl.BlockSpec(shape, lambda *_: (0,) * nd, pipeline_mode=pl.Buffered(1))


def _rms(x):
    return x * lax.rsqrt(jnp.mean(x * x, axis=-1, keepdims=True) + EPS)


def _split3(x):
    hi = x.astype(BF16)
    r = x - hi.astype(F32)
    mid = r.astype(BF16)
    lo = (r - mid.astype(F32)).astype(BF16)
    return hi, mid, lo


def _mod_kernel(c_ref, w_ref, b_ref, o_ref):
    c = c_ref[...]
    c_act = c * (1.0 / (1.0 + jnp.exp(-c)))
    w = w_ref[0].astype(BF16)
    o_ref[0] = jnp.dot(c_act.astype(BF16), w, preferred_element_type=F32) + b_ref[0]


def _mod_vectors(c8, w, b, tn=1024):
    g, d, n = w.shape
    return pl.pallas_call(
        _mod_kernel,
        grid=(g, n // tn),
        in_specs=[pl.BlockSpec((8, d), lambda i, j: (0, 0)),
                  pl.BlockSpec((1, d, tn), lambda i, j: (i, 0, j)),
                  pl.BlockSpec((1, 1, tn), lambda i, j: (i, 0, j))],
        out_specs=pl.BlockSpec((1, 8, tn), lambda i, j: (i, 0, j)),
        out_shape=jax.ShapeDtypeStruct((g, 8, n), F32),
        compiler_params=_params(("parallel", "parallel")),
        name="mod_vectors",
    )(c8, w, b)


def _rope_kernel(pos_ref, inv_ref, cos_ref, sin_ref):
    ang = pos_ref[...].astype(F32) * inv_ref[...]
    cos_ref[...] = jnp.cos(ang)
    sin_ref[...] = jnp.sin(ang)


def _rope_tables(positions):
    half = B_ROPE // 2
    t = positions.size
    rows = t * half // LANES
    per_row = LANES // half
    pos = jnp.repeat(positions.reshape(rows, per_row), half, axis=1)
    inv = ROPE_THETA ** (-jnp.arange(0, B_ROPE, 2, dtype=F32) / B_ROPE)
    inv = jnp.tile(inv, per_row).reshape(1, LANES)
    tr = 1024
    cos, sin = pl.pallas_call(
        _rope_kernel,
        grid=(rows // tr,),
        in_specs=[pl.BlockSpec((tr, LANES), lambda i: (i, 0)),
                  pl.BlockSpec((1, LANES), lambda i: (0, 0))],
        out_specs=[pl.BlockSpec((tr, LANES), lambda i: (i, 0))] * 2,
        out_shape=[jax.ShapeDtypeStruct((rows, LANES), F32)] * 2,
        compiler_params=_params(("parallel",)),
        name="rope_tables",
    )(pos, inv)
    return cos.reshape(t, half), sin.reshape(t, half)


def _log_sigmoid(x):
    return jnp.minimum(x, 0.0) - jnp.log1p(jnp.exp(-jnp.abs(x)))


def _inproj_kernel(x_ref, mod_ref, ng_ref, wqvo_ref, wkt_ref, wg_ref, wgt_ref, bcol_ref, brow_ref,
                   q_ref, kt_ref, v_ref, og_ref, gc_ref, gr_ref):
    m = mod_ref[0]
    xhat = _rms(x_ref[0])
    h = (xhat * (ng_ref[0:1] * (1.0 + m[1:2])) + m[0:1]).astype(BF16)
    nq = A_HEADS * A_DQK
    q = jnp.dot(h, wqvo_ref[:, :nq], preferred_element_type=F32)
    q_ref[0] = (q * (A_DQK ** -0.5)).astype(BF16)
    v = jnp.dot(h, wqvo_ref[:, nq:nq + D_MODEL], preferred_element_type=F32)
    v_ref[0] = v.astype(BF16)
    o = jnp.dot(h, wqvo_ref[:, nq + D_MODEL:], preferred_element_type=F32)
    og_ref[0] = (1.0 / (1.0 + jnp.exp(-o))).astype(BF16)
    nt = (((1,), (1,)), ((), ()))
    kt_ref[0] = lax.dot_general(wkt_ref[...], h, nt, preferred_element_type=F32).astype(BF16)
    gc = jnp.dot(h, wg_ref[...], preferred_element_type=F32) + bcol_ref[...]
    col = lax.broadcasted_iota(jnp.int32, gc.shape, 1)
    gc_ref[0] = jnp.where(col < A_HEADS, gc, _log_sigmoid(gc))
    gr = lax.dot_general(wgt_ref[...], h, nt, preferred_element_type=F32) + brow_ref[...]
    row = lax.broadcasted_iota(jnp.int32, gr.shape, 0)
    gr_ref[0] = jnp.where(row < A_HEADS, gr, _log_sigmoid(gr))


def _inproj(x, modp, ng, wqvo, wkt, wg, wgt, bcol, brow, tm=512):
    b, s, d = x.shape
    nq = A_HEADS * A_DQK
    ng8 = 2 * A_HEADS
    tok = lambda w: pl.BlockSpec((1, tm, w), lambda i, j: (i, j, 0))
    return pl.pallas_call(
        _inproj_kernel,
        grid=(b, s // tm),
        in_specs=[tok(d),
                  pl.BlockSpec((1, 8, d), lambda i, j: (i, 0, 0)),
                  _const_spec(ng.shape), _const_spec(wqvo.shape), _const_spec(wkt.shape),
                  _const_spec(wg.shape), _const_spec(wgt.shape), _const_spec(bcol.shape),
                  _const_spec(brow.shape)],
        out_specs=[tok(nq),
                   pl.BlockSpec((1, nq, tm), lambda i, j: (i, 0, j)),
                   tok(d), tok(d), tok(LANES),
                   pl.BlockSpec((1, ng8, tm), lambda i, j: (i, 0, j))],
        out_shape=[jax.ShapeDtypeStruct((b, s, nq), BF16),
                   jax.ShapeDtypeStruct((b, nq, s), BF16),
                   jax.ShapeDtypeStruct((b, s, d), BF16),
                   jax.ShapeDtypeStruct((b, s, d), BF16),
                   jax.ShapeDtypeStruct((b, s, LANES), F32),
                   jax.ShapeDtypeStruct((b, ng8, s), F32)],
        compiler_params=_params(("parallel", "parallel")),
        name="mlstm_inproj",
    )(x, modp, ng, wqvo, wkt, wg, wgt, bcol, brow)


def _mlstm_kernel(q_ref, kt_ref, v_ref, og_ref, gc_ref, gr_ref, hn_ref, o_ref, c_scr, n_scr, m_scr, *, L):
    @pl.when(pl.program_id(1) == 0)
    def _():
        c_scr[...] = jnp.zeros_like(c_scr)
        n_scr[...] = jnp.zeros_like(n_scr)
        m_scr[...] = jnp.zeros_like(m_scr)

    r = lax.broadcasted_iota(jnp.int32, (L, L), 0)
    cc = lax.broadcasted_iota(jnp.int32, (L, L), 1)
    causal = cc <= r
    tril = jnp.where(causal, 1.0, 0.0).astype(BF16)
    triu = jnp.where(r <= cc, 1.0, 0.0).astype(BF16)
    gc = gc_ref[0]
    gr = gr_ref[0]
    bcol_all = sum(jnp.dot(tril, t, preferred_element_type=F32) for t in _split3(gc))
    brow_all = sum(jnp.dot(t, triu, preferred_element_type=F32) for t in _split3(gr))
    nt = (((1,), (1,)), ((), ()))

    for hd in range(A_HEADS):
        fq = hd + A_HEADS
        i_row = gr[hd:hd + 1, :]
        b_row = brow_all[fq:fq + 1, :]
        i_col = gc[:, hd:hd + 1]
        b_col = bcol_all[:, fq:fq + 1]
        m_prev = m_scr[hd][0:1, 0:1]
        q = q_ref[0, :, hd * A_DQK:(hd + 1) * A_DQK]
        kt = kt_ref[0, hd * A_DQK:(hd + 1) * A_DQK, :]
        v = v_ref[0, :, hd * A_DV:(hd + 1) * A_DV]
        c_prev = c_scr[hd]
        n_prev = n_scr[hd][0:1, :]

        dmat = jnp.where(causal, b_col - b_row + i_row, -jnp.inf)
        inter = b_col + m_prev
        m_t = jnp.maximum(inter, jnp.max(dmat, axis=1, keepdims=True))
        w_inter = jnp.exp(inter - m_t)
        p = jnp.exp(dmat - m_t) * jnp.dot(q, kt, preferred_element_type=F32)
        num = (w_inter * jnp.dot(q, c_prev.astype(BF16), preferred_element_type=F32)
               + jnp.dot(p.astype(BF16), v, preferred_element_type=F32))
        den = (w_inter * jnp.sum(q.astype(F32) * n_prev, axis=1, keepdims=True)
               + jnp.sum(p, axis=1, keepdims=True))
        hh = num / jnp.maximum(jnp.abs(den), jnp.exp(-m_t))
        hh = _rms(hh) * hn_ref[hd:hd + 1, :]
        og = og_ref[0, :, hd * A_DV:(hd + 1) * A_DV].astype(F32)
        o_ref[0, :, hd * A_DV:(hd + 1) * A_DV] = (og * hh).astype(o_ref.dtype)

        b_last = b_row[:, L - 1:L]
        g_row = b_last - b_row + i_row
        m_new = jnp.maximum(b_last + m_prev, jnp.max(g_row, axis=1, keepdims=True))
        decay = jnp.exp(b_last + m_prev - m_new)
        wk = jnp.exp(g_row - m_new)
        kw = (kt.astype(F32) * wk).astype(BF16)
        c_scr[hd] = decay * c_prev + jnp.dot(kw, v, preferred_element_type=F32)
        wk8 = jnp.broadcast_to(wk, (8, L))
        n_add = sum(lax.dot_general(t, kt, nt, preferred_element_type=F32) for t in _split3(wk8))
        n_scr[hd] = decay * n_scr[hd] + n_add
        m_scr[hd] = jnp.broadcast_to(m_new, (8, LANES))


def _mlstm(q, kt, v, og, gc, gr, head_norm, L=128):
    b, s, d = v.shape
    nq = A_HEADS * A_DQK
    ng8 = 2 * A_HEADS
    tok = lambda w: pl.BlockSpec((1, L, w), lambda i, j: (i, j, 0))
    return pl.pallas_call(
        functools.partial(_mlstm_kernel, L=L),
        grid=(b, s // L),
        in_specs=[tok(nq),
                  pl.BlockSpec((1, nq, L), lambda i, j: (i, 0, j)),
                  tok(d), tok(d), tok(LANES),
                  pl.BlockSpec((1, ng8, L), lambda i, j: (i, 0, j)),
                  pl.BlockSpec(head_norm.shape, lambda i, j: (0, 0))],
        out_specs=tok(d),
        out_shape=jax.ShapeDtypeStruct((b, s, d), BF16),
        scratch_shapes=[pltpu.VMEM((A_HEADS, A_DQK, A_DV), F32),
                        pltpu.VMEM((A_HEADS, 8, A_DQK), F32),
                        pltpu.VMEM((A_HEADS, 8, LANES), F32)],
        compiler_params=_params(("parallel", "arbitrary")),
        name="mlstm_chunkwise",
    )(q, kt, v, og, gc, gr, head_norm)


def _post_kernel(a_ref, x_ref, mod_ref, ng_ref, wp_ref, w1_ref, w2_ref, o_ref, *, ff_chunk):
    m = mod_ref[0]
    y = jnp.dot(a_ref[0], wp_ref[...], preferred_element_type=F32)
    x1 = x_ref[0] + (m[2:3] * ng_ref[1:2]) * _rms(y)
    h = (_rms(x1) * (ng_ref[2:3] * (1.0 + m[4:5])) + m[3:4]).astype(BF16)
    acc = jnp.zeros_like(x1)
    for c in range(D_FF // ff_chunk):
        u = jnp.dot(h, w1_ref[:, c * ff_chunk:(c + 1) * ff_chunk], preferred_element_type=F32)
        u = jnp.square(jnp.maximum(u, 0.0)).astype(BF16)
        acc = acc + jnp.dot(u, w2_ref[c * ff_chunk:(c + 1) * ff_chunk, :], preferred_element_type=F32)
    o_ref[0] = x1 + (m[5:6] * ng_ref[3:4]) * _rms(acc)


def _post(a, x, modp, ng, wp, w1, w2, tm=512, ff_chunk=1024):
    b, s, d = x.shape
    tok = pl.BlockSpec((1, tm, d), lambda i, j: (i, j, 0))
    return pl.pallas_call(
        functools.partial(_post_kernel, ff_chunk=ff_chunk),
        grid=(b, s // tm),
        in_specs=[tok, tok,
                  pl.BlockSpec((1, 8, d), lambda i, j: (i, 0, 0)),
                  _const_spec(ng.shape), _const_spec(wp.shape), _const_spec(w1.shape),
                  _const_spec(w2.shape)],
        out_specs=tok,
        out_shape=jax.ShapeDtypeStruct((b, s, d), F32),
        compiler_params=_params(("parallel", "parallel")),
        name="outproj_mlp",
    )(a, x, modp, ng, wp, w1, w2)


def _rope_block(blk, cosp, sinp):
    return blk * cosp + pltpu.roll(blk, LANES // 2, axis=1) * sinp


def _qkv_kernel(x_ref, mod_ref, kvmod_ref, ng_ref, kvn_ref, wqa_ref, qn_ref, wqb_ref, wdkv_ref,
                kln_ref, wukv_ref, cos_ref, sin_ref, q_ref, k_ref, v_ref):
    m = mod_ref[0]
    km = kvmod_ref[0]
    xhat = _rms(x_ref[0])
    cosp = cos_ref[0]
    sinp = sin_ref[0]
    scale = (B_NOPE + B_ROPE) ** -0.5 * LOG2E

    h = (xhat * (ng_ref[0:1] * (1.0 + m[1:2])) + m[0:1]).astype(BF16)
    qa = jnp.dot(h, wqa_ref[...], preferred_element_type=F32)
    qn = (_rms(qa) * qn_ref[...]).astype(BF16)
    q = jnp.dot(qn, wqb_ref[...], preferred_element_type=F32) * scale
    for hd in range(B_HEADS):
        lo = hd * QK_PAD
        q_ref[0, :, lo:lo + B_NOPE] = q[:, lo:lo + B_NOPE].astype(BF16)
        q_ref[0, :, lo + B_NOPE:lo + QK_PAD] = _rope_block(
            q[:, lo + B_NOPE:lo + QK_PAD], cosp, sinp).astype(BF16)

    hs = (xhat * (kvn_ref[...] * (1.0 + km[1:2])) + km[0:1]).astype(BF16)
    dkv = jnp.dot(hs, wdkv_ref[...], preferred_element_type=F32)
    kr = _rope_block(dkv[:, B_KVLORA:], cosp, sinp).astype(BF16)
    ckv = (_rms(dkv[:, :B_KVLORA]) * kln_ref[...]).astype(BF16)
    kv = jnp.dot(ckv, wukv_ref[...], preferred_element_type=F32)
    for hd in range(B_HEADS):
        lo = hd * QK_PAD
        k_ref[0, :, lo:lo + B_NOPE] = kv[:, hd * B_NOPE:(hd + 1) * B_NOPE].astype(BF16)
        k_ref[0, :, lo + B_NOPE:lo + QK_PAD] = kr
    v_ref[0] = kv[:, B_HEADS * B_NOPE:].astype(BF16)


def _qkv(x, modp, kvmodp, ng, kvn, wqa, qnorm, wqb, wdkv, kln, wukv, cosp, sinp, tm=512):
    b, s, d = x.shape
    tok = lambda w: pl.BlockSpec((1, tm, w), lambda i, j: (i, j, 0))
    modspec = pl.BlockSpec((1, 8, d), lambda i, j: (i, 0, 0))
    hq = B_HEADS * QK_PAD
    return pl.pallas_call(
        _qkv_kernel,
        grid=(b, s // tm),
        in_specs=[tok(d), modspec, modspec,
                  _const_spec(ng.shape), _const_spec(kvn.shape), _const_spec(wqa.shape),
                  _const_spec(qnorm.shape), _const_spec(wqb.shape), _const_spec(wdkv.shape),
                  _const_spec(kln.shape), _const_spec(wukv.shape),
                  tok(LANES), tok(LANES)],
        out_specs=[tok(hq), tok(hq), tok(d)],
        out_shape=[jax.ShapeDtypeStruct((b, s, hq), BF16),
                   jax.ShapeDtypeStruct((b, s, hq), BF16),
                   jax.ShapeDtypeStruct((b, s, d), BF16)],
        compiler_params=_params(("parallel", "parallel")),
        name="mla_qkv",
    )(x, modp, kvmodp, ng, kvn, wqa, qnorm, wqb, wdkv, kln, wukv, cosp, sinp)


def _attn_kernel(q_ref, k_ref, v_ref, o_ref, vaug_scr, s_scr, *, tq, tk):
    i = pl.program_id(2)
    s_len = v_ref.shape[1]
    rows = 1024

    @pl.when(i == 0)
    def _():
        lane = lax.broadcasted_iota(jnp.int32, (rows, LANES), 1)
        ones_col = jnp.where(lane == 0, 1.0, 0.0).astype(BF16)

        def fill(r, carry):
            off = pl.multiple_of(r * rows, rows)
            vaug_scr[pl.ds(off, rows), 0:B_VDIM] = v_ref[0, pl.ds(off, rows), :]
            vaug_scr[pl.ds(off, rows), B_VDIM:] = ones_col
            return carry
        lax.fori_loop(0, s_len // rows, fill, 0)

    q = q_ref[0]
    nt = (((1,), (1,)), ((), ()))

    def scores(j, slot):
        off = pl.multiple_of(j * tk, tk)
        s = lax.dot_general(q, k_ref[0, pl.ds(off, tk), :], nt, preferred_element_type=F32)
        s_scr[slot] = s
        return jnp.max(s, axis=1, keepdims=True)

    def update(j, slot, smax, m, acc, masked):
        off = pl.multiple_of(j * tk, tk)
        s = s_scr[slot]
        if masked:
            qc = lax.broadcasted_iota(jnp.int32, (tq, tk), 0) // CHUNK
            kc = lax.broadcasted_iota(jnp.int32, (tq, tk), 1) // CHUNK
            s = jnp.where(kc <= qc, s, -jnp.inf)
            smax = jnp.max(s, axis=1, keepdims=True)
        m_new = jnp.maximum(m, smax)
        alpha = jnp.exp2(m - m_new)
        p = jnp.exp2(s - m_new).astype(BF16)
        acc = alpha * acc + jnp.dot(p, vaug_scr[pl.ds(off, tk), :], preferred_element_type=F32)
        return m_new, acc

    def finish(acc):
        o_ref[0] = (acc[:, :B_VDIM] / acc[:, B_VDIM:B_VDIM + 1]).astype(o_ref.dtype)

    def pair(t, carry):
        m, acc, smax0 = carry
        j = 2 * t
        smax1 = scores(j + 1, 1)
        m, acc = update(j, 0, smax0, m, acc, False)
        smax0 = scores(j + 2, 0)
        m, acc = update(j + 1, 1, smax1, m, acc, False)
        return m, acc, smax0

    init = (jnp.full((tq, 1), -jnp.inf, F32), jnp.zeros((tq, 2 * B_VDIM), F32), scores(0, 0))
    m, acc, smax0 = lax.fori_loop(0, i // 2, pair, init)

    @pl.when(i % 2 == 0)
    def _():
        finish(update(i, 0, smax0, m, acc, True)[1])

    @pl.when(i % 2 == 1)
    def _():
        smax1 = scores(i, 1)
        m1, acc1 = update(i - 1, 0, smax0, m, acc, False)
        finish(update(i, 1, smax1, m1, acc1, True)[1])


def _attention(q, k, v, tq=512, tk=512):
    b, s, _ = q.shape
    return pl.pallas_call(
        functools.partial(_attn_kernel, tq=tq, tk=tk),
        grid=(b, B_HEADS, s // tq),
        in_specs=[pl.BlockSpec((1, tq, QK_PAD), lambda bi, h, i: (bi, i, h)),
                  pl.BlockSpec((1, s, QK_PAD), lambda bi, h, i: (bi, 0, h)),
                  pl.BlockSpec((1, s, B_VDIM), lambda bi, h, i: (bi, 0, h))],
        out_specs=pl.BlockSpec((1, tq, B_VDIM), lambda bi, h, i: (bi, i, h)),
        out_shape=jax.ShapeDtypeStruct((b, s, B_HEADS * B_VDIM), BF16),
        scratch_shapes=[pltpu.VMEM((s, 2 * B_VDIM), BF16), pltpu.VMEM((2, tq, tk), F32)],
        compiler_params=_params(("parallel", "parallel", "arbitrary")),
        name="mla_attention",
    )(q, k, v)


def _pack_mod(vec6, b):
    m = vec6[:b].reshape(b, 6, D_MODEL)
    return jnp.pad(m, ((0, 0), (0, 2), (0, 0)))


def _rope_cols(w, n_lead):
    half = B_ROPE // 2
    z = jnp.zeros((w.shape[0], half), w.dtype)
    return jnp.concatenate([w[:, :n_lead], w[:, n_lead:n_lead + half], z, w[:, n_lead + half:], z], axis=1)


def kernel(x, c, positions, mod_w, mod_b, norm_g, ffn_w1, ffn_w2, a_w_in, a_b_gates, a_head_norm,
           a_w_out, b_w_qa, b_q_norm, b_w_qb, b_w_o, kv_mod_w, kv_mod_b, kv_norm, w_dkv,
           kv_lora_norm, w_ukv):
    b, s, d = x.shape

    c8 = jnp.pad(c, ((0, 8 - b), (0, 0)))
    mods = _mod_vectors(c8, mod_w, mod_b[:, None, :])
    kvmod = _mod_vectors(c8, kv_mod_w[None], kv_mod_b[None, None, :])[0]
    modp0 = _pack_mod(mods[0], b)
    modp1 = _pack_mod(mods[1], b)
    kvmodp = jnp.pad(kvmod[:b].reshape(b, 2, d), ((0, 0), (0, 6), (0, 0)))

    w_in = a_w_in[0]
    nq = A_HEADS * A_DQK
    wq, wk, wv, wo, wgate = (w_in[:, :nq], w_in[:, nq:2 * nq], w_in[:, 2 * nq:2 * nq + d],
                             w_in[:, 2 * nq + d:2 * nq + 2 * d], w_in[:, 2 * nq + 2 * d:])
    wqvo = jnp.concatenate([wq, wv, wo], axis=1).astype(BF16)
    wkt = wk.T.astype(BF16)
    ng8 = 2 * A_HEADS
    wg = jnp.pad(wgate, ((0, 0), (0, LANES - ng8))).astype(BF16)
    wgt = wgate.T.astype(BF16)
    bcol = jnp.pad(a_b_gates[0], (0, LANES - ng8)).reshape(1, LANES)
    brow = a_b_gates[0].reshape(ng8, 1)
    q, kt, v, og, gc, gr = _inproj(x, modp0, norm_g[0], wqvo, wkt, wg, wgt, bcol, brow)
    a = _mlstm(q, kt, v, og, gc, gr, a_head_norm[0])
    x = _post(a, x, modp0, norm_g[0], a_w_out[0].astype(BF16), ffn_w1[0].astype(BF16),
              ffn_w2[0].astype(BF16))

    cos, sin = _rope_tables(positions)
    cosp = jnp.tile(cos, (1, 4)).reshape(b, s, LANES)
    sinp = jnp.concatenate([-sin, -sin, sin, sin], axis=1).reshape(b, s, LANES)
    wqb = b_w_qb[0].reshape(B_QLORA * B_HEADS, B_NOPE + B_ROPE)
    wqb = _rope_cols(wqb, B_NOPE).reshape(B_QLORA, B_HEADS * QK_PAD).astype(BF16)
    wdkv = _rope_cols(w_dkv, B_KVLORA).astype(BF16)
    wukv = w_ukv.reshape(B_KVLORA, B_HEADS, B_NOPE + B_VDIM)
    wukv = jnp.concatenate([wukv[:, :, :B_NOPE].reshape(B_KVLORA, -1),
                            wukv[:, :, B_NOPE:].reshape(B_KVLORA, -1)], axis=1).astype(BF16)
    qh, kh, vh = _qkv(x, modp1, kvmodp, norm_g[1], kv_norm.reshape(1, d), b_w_qa[0].astype(BF16),
                      b_q_norm[0].reshape(1, -1), wqb, wdkv, kv_lora_norm.reshape(1, -1), wukv,
                      cosp, sinp)
    a = _attention(qh, kh, vh)
    x = _post(a, x, modp1, norm_g[1], b_w_o[0].astype(BF16), ffn_w1[1].astype(BF16),
              ffn_w2[1].astype(BF16))
    return x
```

```python
import functools

import jax
import jax.numpy as jnp
from jax import lax
from jax.experimental import pallas as pl
from jax.experimental.pallas import tpu as pltpu

D_MODEL = 1024
CHUNK = 64
EPS = 1e-6
D_FF = 4 * D_MODEL
A_HEADS = 4
A_DQK = 128
A_DV = D_MODEL // A_HEADS
B_HEADS = 8
B_NOPE = 128
B_ROPE = 64
B_VDIM = D_MODEL // B_HEADS
B_QLORA = 384
B_KVLORA = 256
ROPE_THETA = 10000.0

LANES = 128
QK_PAD = 256
VMEM_LIMIT = 56 * 1024 * 1024
LOG2E = 1.4426950408889634

F32 = jnp.float32
BF16 = jnp.bfloat16


def _params(sem, vmem=VMEM_LIMIT):
    return pltpu.CompilerParams(dimension_semantics=sem, vmem_limit_bytes=vmem)


def _const_spec(shape):
    nd = len(shape)
    return pl.BlockSpec(shape, lambda *_: (0,) * nd, pipeline_mode=pl.Buffered(1))


def _rms(x):
    return x * lax.rsqrt(jnp.mean(x * x, axis=-1, keepdims=True) + EPS)


def _split3(x):
    hi = x.astype(BF16)
    r = x - hi.astype(F32)
    mid = r.astype(BF16)
    lo = (r - mid.astype(F32)).astype(BF16)
    return hi, mid, lo


def _mod_kernel(c_ref, w_ref, b_ref, o_ref):
    c = c_ref[...]
    c_act = c * (1.0 / (1.0 + jnp.exp(-c)))
    w = w_ref[0].astype(BF16)
    o_ref[0] = jnp.dot(c_act.astype(BF16), w, preferred_element_type=F32) + b_ref[0]


def _mod_vectors(c8, w, b, tn=1024):
    g, d, n = w.shape
    return pl.pallas_call(
        _mod_kernel,
        grid=(g, n // tn),
        in_specs=[pl.BlockSpec((8, d), lambda i, j: (0, 0)),
                  pl.BlockSpec((1, d, tn), lambda i, j: (i, 0, j)),
                  pl.BlockSpec((1, 1, tn), lambda i, j: (i, 0, j))],
        out_specs=pl.BlockSpec((1, 8, tn), lambda i, j: (i, 0, j)),
        out_shape=jax.ShapeDtypeStruct((g, 8, n), F32),
        compiler_params=_params(("parallel", "parallel")),
        name="mod_vectors",
    )(c8, w, b)


def _rope_kernel(pos_ref, inv_ref, cos_ref, sin_ref):
    ang = pos_ref[...].astype(F32) * inv_ref[...]
    cos_ref[...] = jnp.cos(ang)
    sin_ref[...] = jnp.sin(ang)


def _rope_tables(positions):
    half = B_ROPE // 2
    t = positions.size
    rows = t * half // LANES
    per_row = LANES // half
    pos = jnp.repeat(positions.reshape(rows, per_row), half, axis=1)
    inv = ROPE_THETA ** (-jnp.arange(0, B_ROPE, 2, dtype=F32) / B_ROPE)
    inv = jnp.tile(inv, per_row).reshape(1, LANES)
    tr = 1024
    cos, sin = pl.pallas_call(
        _rope_kernel,
        grid=(rows // tr,),
        in_specs=[pl.BlockSpec((tr, LANES), lambda i: (i, 0)),
                  pl.BlockSpec((1, LANES), lambda i: (0, 0))],
        out_specs=[pl.BlockSpec((tr, LANES), lambda i: (i, 0))] * 2,
        out_shape=[jax.ShapeDtypeStruct((rows, LANES), F32)] * 2,
        compiler_params=_params(("parallel",)),
        name="rope_tables",
    )(pos, inv)
    return cos.reshape(t, half), sin.reshape(t, half)


def _log_sigmoid(x):
    return jnp.minimum(x, 0.0) - jnp.log1p(jnp.exp(-jnp.abs(x)))


def _inproj_kernel(x_ref, mod_ref, ng_ref, wqvo_ref, wkt_ref, wg_ref, wgt_ref, bcol_ref, brow_ref,
                   q_ref, kt_ref, v_ref, og_ref, gc_ref, gr_ref):
    m = mod_ref[0]
    xhat = _rms(x_ref[0])
    h = (xhat * (ng_ref[0:1] * (1.0 + m[1:2])) + m[0:1]).astype(BF16)
    nq = A_HEADS * A_DQK
    q = jnp.dot(h, wqvo_ref[:, :nq], preferred_element_type=F32)
    q_ref[0] = (q * (A_DQK ** -0.5)).astype(BF16)
    v = jnp.dot(h, wqvo_ref[:, nq:nq + D_MODEL], preferred_element_type=F32)
    v_ref[0] = v.astype(BF16)
    o = jnp.dot(h, wqvo_ref[:, nq + D_MODEL:], preferred_element_type=F32)
    og_ref[0] = (1.0 / (1.0 + jnp.exp(-o))).astype(BF16)
    nt = (((1,), (1,)), ((), ()))
    kt_ref[0] = lax.dot_general(wkt_ref[...], h, nt, preferred_element_type=F32).astype(BF16)
    gc = jnp.dot(h, wg_ref[...], preferred_element_type=F32) + bcol_ref[...]
    col = lax.broadcasted_iota(jnp.int32, gc.shape, 1)
    gc_ref[0] = jnp.where(col < A_HEADS, gc, _log_sigmoid(gc))
    gr = lax.dot_general(wgt_ref[...], h, nt, preferred_element_type=F32) + brow_ref[...]
    row = lax.broadcasted_iota(jnp.int32, gr.shape, 0)
    gr_ref[0] = jnp.where(row < A_HEADS, gr, _log_sigmoid(gr))


def _inproj(x, modp, ng, wqvo, wkt, wg, wgt, bcol, brow, tm=512):
    b, s, d = x.shape
    nq = A_HEADS * A_DQK
    ng8 = 2 * A_HEADS
    tok = lambda w: pl.BlockSpec((1, tm, w), lambda i, j: (i, j, 0))
    return pl.pallas_call(
        _inproj_kernel,
        grid=(b, s // tm),
        in_specs=[tok(d),
                  pl.BlockSpec((1, 8, d), lambda i, j: (i, 0, 0)),
                  _const_spec(ng.shape), _const_spec(wqvo.shape), _const_spec(wkt.shape),
                  _const_spec(wg.shape), _const_spec(wgt.shape), _const_spec(bcol.shape),
                  _const_spec(brow.shape)],
        out_specs=[tok(nq),
                   pl.BlockSpec((1, nq, tm), lambda i, j: (i, 0, j)),
                   tok(d), tok(d), tok(LANES),
                   pl.BlockSpec((1, ng8, tm), lambda i, j: (i, 0, j))],
        out_shape=[jax.ShapeDtypeStruct((b, s, nq), BF16),
                   jax.ShapeDtypeStruct((b, nq, s), BF16),
                   jax.ShapeDtypeStruct((b, s, d), BF16),
                   jax.ShapeDtypeStruct((b, s, d), BF16),
                   jax.ShapeDtypeStruct((b, s, LANES), F32),
                   jax.ShapeDtypeStruct((b, ng8, s), F32)],
        compiler_params=_params(("parallel", "parallel")),
        name="mlstm_inproj",
    )(x, modp, ng, wqvo, wkt, wg, wgt, bcol, brow)


def _mlstm_kernel(q_ref, kt_ref, v_ref, og_ref, gc_ref, gr_ref, hn_ref, o_ref, c_scr, n_scr, m_scr, *, L):
    @pl.when(pl.program_id(1) == 0)
    def _():
        c_scr[...] = jnp.zeros_like(c_scr)
        n_scr[...] = jnp.zeros_like(n_scr)
        m_scr[...] = jnp.zeros_like(m_scr)

    r = lax.broadcasted_iota(jnp.int32, (L, L), 0)
    cc = lax.broadcasted_iota(jnp.int32, (L, L), 1)
    causal = cc <= r
    tril = jnp.where(causal, 1.0, 0.0).astype(BF16)
    triu = jnp.where(r <= cc, 1.0, 0.0).astype(BF16)
    gc = gc_ref[0]
    gr = gr_ref[0]
    bcol_all = sum(jnp.dot(tril, t, preferred_element_type=F32) for t in _split3(gc))
    brow_all = sum(jnp.dot(t, triu, preferred_element_type=F32) for t in _split3(gr))
    nt = (((1,), (1,)), ((), ()))

    for hd in range(A_HEADS):
        fq = hd + A_HEADS
        i_row = gr[hd:hd + 1, :]
        b_row = brow_all[fq:fq + 1, :]
        i_col = gc[:, hd:hd + 1]
        b_col = bcol_all[:, fq:fq + 1]
        m_prev = m_scr[hd][0:1, 0:1]
        q = q_ref[0, :, hd * A_DQK:(hd + 1) * A_DQK]
        kt = kt_ref[0, hd * A_DQK:(hd + 1) * A_DQK, :]
        v = v_ref[0, :, hd * A_DV:(hd + 1) * A_DV]
        c_prev = c_scr[hd]
        n_prev = n_scr[hd][0:1, :]

        dmat = jnp.where(causal, b_col - b_row + i_row, -jnp.inf)
        inter = b_col + m_prev
        m_t = jnp.maximum(inter, jnp.max(dmat, axis=1, keepdims=True))
        w_inter = jnp.exp(inter - m_t)
        p = jnp.exp(dmat - m_t) * jnp.dot(q, kt, preferred_element_type=F32)
        num = (w_inter * jnp.dot(q, c_prev.astype(BF16), preferred_element_type=F32)
               + jnp.dot(p.astype(BF16), v, preferred_element_type=F32))
        den = (w_inter * jnp.sum(q.astype(F32) * n_prev, axis=1, keepdims=True)
               + jnp.sum(p, axis=1, keepdims=True))
        hh = num / jnp.maximum(jnp.abs(den), jnp.exp(-m_t))
        hh = _rms(hh) * hn_ref[hd:hd + 1, :]
        og = og_ref[0, :, hd * A_DV:(hd + 1) * A_DV].astype(F32)
        o_ref[0, :, hd * A_DV:(hd + 1) * A_DV] = (og * hh).astype(o_ref.dtype)

        b_last = b_row[:, L - 1:L]
        g_row = b_last - b_row + i_row
        m_new = jnp.maximum(b_last + m_prev, jnp.max(g_row, axis=1, keepdims=True))
        decay = jnp.exp(b_last + m_prev - m_new)
        wk = jnp.exp(g_row - m_new)
        kw = (kt.astype(F32) * wk).astype(BF16)
        c_scr[hd] = decay * c_prev + jnp.dot(kw, v, preferred_element_type=F32)
        wk8 = jnp.broadcast_to(wk, (8, L))
        n_add = sum(lax.dot_general(t, kt, nt, preferred_element_type=F32) for t in _split3(wk8))
        n_scr[hd] = decay * n_scr[hd] + n_add
        m_scr[hd] = jnp.broadcast_to(m_new, (8, LANES))


def _mlstm(q, kt, v, og, gc, gr, head_norm, L=128):
    b, s, d = v.shape
    nq = A_HEADS * A_DQK
    ng8 = 2 * A_HEADS
    tok = lambda w: pl.BlockSpec((1, L, w), lambda i, j: (i, j, 0))
    return pl.pallas_call(
        functools.partial(_mlstm_kernel, L=L),
        grid=(b, s // L),
        in_specs=[tok(nq),
                  pl.BlockSpec((1, nq, L), lambda i, j: (i, 0, j)),
                  tok(d), tok(d), tok(LANES),
                  pl.BlockSpec((1, ng8, L), lambda i, j: (i, 0, j)),
                  pl.BlockSpec(head_norm.shape, lambda i, j: (0, 0))],
        out_specs=tok(d),
        out_shape=jax.ShapeDtypeStruct((b, s, d), BF16),
        scratch_shapes=[pltpu.VMEM((A_HEADS, A_DQK, A_DV), F32),
                        pltpu.VMEM((A_HEADS, 8, A_DQK), F32),
                        pltpu.VMEM((A_HEADS, 8, LANES), F32)],
        compiler_params=_params(("parallel", "arbitrary")),
        name="mlstm_chunkwise",
    )(q, kt, v, og, gc, gr, head_norm)


def _post_kernel(a_ref, x_ref, mod_ref, ng_ref, wp_ref, w1_ref, w2_ref, o_ref, *, ff_chunk):
    m = mod_ref[0]
    y = jnp.dot(a_ref[0], wp_ref[...], preferred_element_type=F32)
    x1 = x_ref[0] + (m[2:3] * ng_ref[1:2]) * _rms(y)
    h = (_rms(x1) * (ng_ref[2:3] * (1.0 + m[4:5])) + m[3:4]).astype(BF16)
    acc = jnp.zeros_like(x1)
    for c in range(D_FF // ff_chunk):
        u = jnp.dot(h, w1_ref[:, c * ff_chunk:(c + 1) * ff_chunk], preferred_element_type=F32)
        u = jnp.square(jnp.maximum(u, 0.0)).astype(BF16)
        acc = acc + jnp.dot(u, w2_ref[c * ff_chunk:(c + 1) * ff_chunk, :], preferred_element_type=F32)
    o_ref[0] = x1 + (m[5:6] * ng_ref[3:4]) * _rms(acc)


def _post(a, x, modp, ng, wp, w1, w2, tm=512, ff_chunk=1024):
    b, s, d = x.shape
    tok = pl.BlockSpec((1, tm, d), lambda i, j: (i, j, 0))
    return pl.pallas_call(
        functools.partial(_post_kernel, ff_chunk=ff_chunk),
        grid=(b, s // tm),
        in_specs=[tok, tok,
                  pl.BlockSpec((1, 8, d), lambda i, j: (i, 0, 0)),
                  _const_spec(ng.shape), _const_spec(wp.shape), _const_spec(w1.shape),
                  _const_spec(w2.shape)],
        out_specs=tok,
        out_shape=jax.ShapeDtypeStruct((b, s, d), F32),
        compiler_params=_params(("parallel", "parallel")),
        name="outproj_mlp",
    )(a, x, modp, ng, wp, w1, w2)


def _rope_block(blk, cosp, sinp):
    return blk * cosp + pltpu.roll(blk, LANES // 2, axis=1) * sinp


def _qkv_kernel(x_ref, mod_ref, kvmod_ref, ng_ref, kvn_ref, wqa_ref, qn_ref, wqb_ref, wdkv_ref,
                kln_ref, wukv_ref, cos_ref, sin_ref, q_ref, k_ref, v_ref):
    m = mod_ref[0]
    km = kvmod_ref[0]
    xhat = _rms(x_ref[0])
    cosp = cos_ref[0]
    sinp = sin_ref[0]
    scale = (B_NOPE + B_ROPE) ** -0.5 * LOG2E

    h = (xhat * (ng_ref[0:1] * (1.0 + m[1:2])) + m[0:1]).astype(BF16)
    qa = jnp.dot(h, wqa_ref[...], preferred_element_type=F32)
    qn = (_rms(qa) * qn_ref[...]).astype(BF16)
    q = jnp.dot(qn, wqb_ref[...], preferred_element_type=F32) * scale
    for hd in range(B_HEADS):
        lo = hd * QK_PAD
        q_ref[0, :, lo:lo + B_NOPE] = q[:, lo:lo + B_NOPE].astype(BF16)
        q_ref[0, :, lo + B_NOPE:lo + QK_PAD] = _rope_block(
            q[:, lo + B_NOPE:lo + QK_PAD], cosp, sinp).astype(BF16)

    hs = (xhat * (kvn_ref[...] * (1.0 + km[1:2])) + km[0:1]).astype(BF16)
    dkv = jnp.dot(hs, wdkv_ref[...], preferred_element_type=F32)
    kr = _rope_block(dkv[:, B_KVLORA:], cosp, sinp).astype(BF16)
    ckv = (_rms(dkv[:, :B_KVLORA]) * kln_ref[...]).astype(BF16)
    kv = jnp.dot(ckv, wukv_ref[...], preferred_element_type=F32)
    for hd in range(B_HEADS):
        lo = hd * QK_PAD
        k_ref[0, :, lo:lo + B_NOPE] = kv[:, hd * B_NOPE:(hd + 1) * B_NOPE].astype(BF16)
        k_ref[0, :, lo + B_NOPE:lo + QK_PAD] = kr
    v_ref[0] = kv[:, B_HEADS * B_NOPE:].astype(BF16)


def _qkv(x, modp, kvmodp, ng, kvn, wqa, qnorm, wqb, wdkv, kln, wukv, cosp, sinp, tm=512):
    b, s, d = x.shape
    tok = lambda w: pl.BlockSpec((1, tm, w), lambda i, j: (i, j, 0))
    modspec = pl.BlockSpec((1, 8, d), lambda i, j: (i, 0, 0))
    hq = B_HEADS * QK_PAD
    return pl.pallas_call(
        _qkv_kernel,
        grid=(b, s // tm),
        in_specs=[tok(d), modspec, modspec,
                  _const_spec(ng.shape), _const_spec(kvn.shape), _const_spec(wqa.shape),
                  _const_spec(qnorm.shape), _const_spec(wqb.shape), _const_spec(wdkv.shape),
                  _const_spec(kln.shape), _const_spec(wukv.shape),
                  tok(LANES), tok(LANES)],
        out_specs=[tok(hq), tok(hq), tok(d)],
        out_shape=[jax.ShapeDtypeStruct((b, s, hq), BF16),
                   jax.ShapeDtypeStruct((b, s, hq), BF16),
                   jax.ShapeDtypeStruct((b, s, d), BF16)],
        compiler_params=_params(("parallel", "parallel")),
        name="mla_qkv",
    )(x, modp, kvmodp, ng, kvn, wqa, qnorm, wqb, wdkv, kln, wukv, cosp, sinp)


def _attn_kernel(q_ref, k_ref, v_ref, o_ref, vaug_scr, s_scr, smax_scr, m_scr, acc_scr, *, tk):
    i = pl.program_id(2)
    s_len = v_ref.shape[1]
    tq = 2 * tk
    rows = 1024

    @pl.when(i == 0)
    def _():
        lane = lax.broadcasted_iota(jnp.int32, (rows, LANES), 1)
        ones_col = jnp.where(lane == 0, 1.0, 0.0).astype(BF16)

        def fill(r, carry):
            off = pl.multiple_of(r * rows, rows)
            vaug_scr[pl.ds(off, rows), 0:B_VDIM] = v_ref[0, pl.ds(off, rows), :]
            vaug_scr[pl.ds(off, rows), B_VDIM:] = ones_col
            return carry
        lax.fori_loop(0, s_len // rows, fill, 0)

    m_scr[...] = jnp.full(m_scr.shape, -jnp.inf, F32)
    acc_scr[...] = jnp.zeros(acc_scr.shape, F32)
    q = q_ref[0]
    nt = (((1,), (1,)), ((), ()))
    lower = slice(0, tk)
    upper = slice(tk, tq)
    every = slice(0, tq)

    def scores(j, slot):
        off = pl.multiple_of(j * tk, tk)
        s = lax.dot_general(q, k_ref[0, pl.ds(off, tk), :], nt, preferred_element_type=F32)
        s_scr[slot] = s
        smax_scr[slot] = jnp.broadcast_to(jnp.max(s, axis=1, keepdims=True), (tq, LANES))

    def update(j, slot, rows_, diagonal):
        off = pl.multiple_of(j * tk, tk)
        s = s_scr[slot, rows_, :]
        if diagonal:
            qc = lax.broadcasted_iota(jnp.int32, s.shape, 0) // CHUNK
            kc = lax.broadcasted_iota(jnp.int32, s.shape, 1) // CHUNK
            s = jnp.where(kc <= qc, s, -jnp.inf)
            smax = jnp.broadcast_to(jnp.max(s, axis=1, keepdims=True), (s.shape[0], LANES))
        else:
            smax = smax_scr[slot, rows_, :]
        m = m_scr[rows_, :]
        m_new = jnp.maximum(m, smax)
        alpha = jnp.exp2(m - m_new)
        p = jnp.exp2(s - jnp.tile(m_new, (1, tk // LANES))).astype(BF16)
        pv = jnp.dot(p, vaug_scr[pl.ds(off, tk), :], preferred_element_type=F32)
        acc_scr[rows_, :] = jnp.tile(alpha, (1, 2 * B_VDIM // LANES)) * acc_scr[rows_, :] + pv
        m_scr[rows_, :] = m_new

    scores(0, 0)
    scores(1, 1)

    def pair(t, carry):
        j = 2 * t
        update(j, 0, every, False)
        scores(j + 2, 0)
        update(j + 1, 1, every, False)
        scores(j + 3, 1)
        return carry
    lax.fori_loop(0, i, pair, 0)

    update(2 * i, 0, lower, True)
    update(2 * i, 0, upper, False)
    update(2 * i + 1, 1, upper, True)
    acc = acc_scr[...]
    o_ref[0] = (acc[:, :B_VDIM] / acc[:, B_VDIM:B_VDIM + 1]).astype(o_ref.dtype)


def _attention(q, k, v, tk=512):
    b, s, _ = q.shape
    tq = 2 * tk
    return pl.pallas_call(
        functools.partial(_attn_kernel, tk=tk),
        grid=(b, B_HEADS, s // tq),
        in_specs=[pl.BlockSpec((1, tq, QK_PAD), lambda bi, h, i: (bi, i, h)),
                  pl.BlockSpec((1, s, QK_PAD), lambda bi, h, i: (bi, 0, h)),
                  pl.BlockSpec((1, s, B_VDIM), lambda bi, h, i: (bi, 0, h))],
        out_specs=pl.BlockSpec((1, tq, B_VDIM), lambda bi, h, i: (bi, i, h)),
        out_shape=jax.ShapeDtypeStruct((b, s, B_HEADS * B_VDIM), BF16),
        scratch_shapes=[pltpu.VMEM((s, 2 * B_VDIM), BF16),
                        pltpu.VMEM((2, tq, tk), F32),
                        pltpu.VMEM((2, tq, LANES), F32),
                        pltpu.VMEM((tq, LANES), F32),
                        pltpu.VMEM((tq, 2 * B_VDIM), F32)],
        compiler_params=_params(("parallel", "parallel", "arbitrary")),
        name="mla_attention",
    )(q, k, v)


def _pack_mod(vec6, b):
    m = vec6[:b].reshape(b, 6, D_MODEL)
    return jnp.pad(m, ((0, 0), (0, 2), (0, 0)))


def _rope_cols(w, n_lead):
    half = B_ROPE // 2
    z = jnp.zeros((w.shape[0], half), w.dtype)
    return jnp.concatenate([w[:, :n_lead], w[:, n_lead:n_lead + half], z, w[:, n_lead + half:], z], axis=1)


def kernel(x, c, positions, mod_w, mod_b, norm_g, ffn_w1, ffn_w2, a_w_in, a_b_gates, a_head_norm,
           a_w_out, b_w_qa, b_q_norm, b_w_qb, b_w_o, kv_mod_w, kv_mod_b, kv_norm, w_dkv,
           kv_lora_norm, w_ukv):
    b, s, d = x.shape

    c8 = jnp.pad(c, ((0, 8 - b), (0, 0)))
    mods = _mod_vectors(c8, mod_w, mod_b[:, None, :])
    kvmod = _mod_vectors(c8, kv_mod_w[None], kv_mod_b[None, None, :])[0]
    modp0 = _pack_mod(mods[0], b)
    modp1 = _pack_mod(mods[1], b)
    kvmodp = jnp.pad(kvmod[:b].reshape(b, 2, d), ((0, 0), (0, 6), (0, 0)))

    w_in = a_w_in[0]
    nq = A_HEADS * A_DQK
    wq, wk, wv, wo, wgate = (w_in[:, :nq], w_in[:, nq:2 * nq], w_in[:, 2 * nq:2 * nq + d],
                             w_in[:, 2 * nq + d:2 * nq + 2 * d], w_in[:, 2 * nq + 2 * d:])
    wqvo = jnp.concatenate([wq, wv, wo], axis=1).astype(BF16)
    wkt = wk.T.astype(BF16)
    ng8 = 2 * A_HEADS
    wg = jnp.pad(wgate, ((0, 0), (0, LANES - ng8))).astype(BF16)
    wgt = wgate.T.astype(BF16)
    bcol = jnp.pad(a_b_gates[0], (0, LANES - ng8)).reshape(1, LANES)
    brow = a_b_gates[0].reshape(ng8, 1)
    q, kt, v, og, gc, gr = _inproj(x, modp0, norm_g[0], wqvo, wkt, wg, wgt, bcol, brow)
    a = _mlstm(q, kt, v, og, gc, gr, a_head_norm[0])
    x = _post(a, x, modp0, norm_g[0], a_w_out[0].astype(BF16), ffn_w1[0].astype(BF16),
              ffn_w2[0].astype(BF16))

    cos, sin = _rope_tables(positions)
    cosp = jnp.tile(cos, (1, 4)).reshape(b, s, LANES)
    sinp = jnp.concatenate([-sin, -sin, sin, sin], axis=1).reshape(b, s, LANES)
    wqb = b_w_qb[0].reshape(B_QLORA * B_HEADS, B_NOPE + B_ROPE)
    wqb = _rope_cols(wqb, B_NOPE).reshape(B_QLORA, B_HEADS * QK_PAD).astype(BF16)
    wdkv = _rope_cols(w_dkv, B_KVLORA).astype(BF16)
    wukv = w_ukv.reshape(B_KVLORA, B_HEADS, B_NOPE + B_VDIM)
    wukv = jnp.concatenate([wukv[:, :, :B_NOPE].reshape(B_KVLORA, -1),
                            wukv[:, :, B_NOPE:].reshape(B_KVLORA, -1)], axis=1).astype(BF16)
    qh, kh, vh = _qkv(x, modp1, kvmodp, norm_g[1], kv_norm.reshape(1, d), b_w_qa[0].astype(BF16),
                      b_q_norm[0].reshape(1, -1), wqb, wdkv, kv_lora_norm.reshape(1, -1), wukv,
                      cosp, sinp)
    a = _attention(qh, kh, vh)
    x = _post(a, x, modp1, norm_g[1], b_w_o[0].astype(BF16), ffn_w1[1].astype(BF16),
              ffn_w2[1].astype(BF16))
    return x
```

```python
import functools

import jax
import jax.numpy as jnp
from jax import lax
from jax.experimental import pallas as pl
from jax.experimental.pallas import tpu as pltpu

D_MODEL = 1024
CHUNK = 64
EPS = 1e-6
D_FF = 4 * D_MODEL
A_HEADS = 4
A_DQK = 128
A_DV = D_MODEL // A_HEADS
B_HEADS = 8
B_NOPE = 128
B_ROPE = 64
B_VDIM = D_MODEL // B_HEADS
B_QLORA = 384
B_KVLORA = 256
ROPE_THETA = 10000.0

LANES = 128
QK_PAD = 256
VMEM_LIMIT = 56 * 1024 * 1024
LOG2E = 1.4426950408889634

F32 = jnp.float32
BF16 = jnp.bfloat16


def _params(sem, vmem=VMEM_LIMIT):
    return pltpu.CompilerParams(dimension_semantics=sem, vmem_limit_bytes=vmem)


def _const_spec(shape):
    nd = len(shape)
    return pl.BlockSpec(shape, lambda *_: (0,) * nd, pipeline_mode=pl.Buffered(1))


def _rms(x):
    return x * lax.rsqrt(jnp.mean(x * x, axis=-1, keepdims=True) + EPS)


def _split3(x):
    hi = x.astype(BF16)
    r = x - hi.astype(F32)
    mid = r.astype(BF16)
    lo = (r - mid.astype(F32)).astype(BF16)
    return hi, mid, lo


def _mod_kernel(c_ref, w_ref, b_ref, o_ref):
    c = c_ref[...]
    c_act = c * (1.0 / (1.0 + jnp.exp(-c)))
    w = w_ref[0].astype(BF16)
    o_ref[0] = jnp.dot(c_act.astype(BF16), w, preferred_element_type=F32) + b_ref[0]


def _mod_vectors(c8, w, b, tn=1024):
    g, d, n = w.shape
    return pl.pallas_call(
        _mod_kernel,
        grid=(g, n // tn),
        in_specs=[pl.BlockSpec((8, d), lambda i, j: (0, 0)),
                  pl.BlockSpec((1, d, tn), lambda i, j: (i, 0, j)),
                  pl.BlockSpec((1, 1, tn), lambda i, j: (i, 0, j))],
        out_specs=pl.BlockSpec((1, 8, tn), lambda i, j: (i, 0, j)),
        out_shape=jax.ShapeDtypeStruct((g, 8, n), F32),
        compiler_params=_params(("parallel", "parallel")),
        name="mod_vectors",
    )(c8, w, b)


def _rope_kernel(pos_ref, inv_ref, sign_ref, cos_ref, sin_ref):
    half = B_ROPE // 2
    per_row = LANES // half
    tr = pos_ref.shape[0]
    pos = pos_ref[...].astype(F32)
    lane_tok = lax.broadcasted_iota(jnp.int32, (tr, LANES), 1) // half
    pos_d = jnp.zeros((tr, LANES), F32)
    for j in range(per_row):
        pos_d = jnp.where(lane_tok == j, pos[:, j:j + 1], pos_d)
    ang = pos_d * inv_ref[...]
    c = jnp.cos(ang)
    sn = jnp.sin(ang)
    for j in range(per_row):
        cj = c[:, j * half:(j + 1) * half]
        sj = sn[:, j * half:(j + 1) * half]
        cos_ref[pl.ds(j, tr, stride=per_row), :] = jnp.concatenate([cj] * per_row, axis=1)
        sin_ref[pl.ds(j, tr, stride=per_row), :] = jnp.concatenate([sj] * per_row, axis=1) * sign_ref[...]


def _rope_tables(positions, tr=512):
    b, s = positions.shape
    half = B_ROPE // 2
    per_row = LANES // half
    rows = b * s // per_row
    inv = ROPE_THETA ** (-jnp.arange(0, B_ROPE, 2, dtype=F32) / B_ROPE)
    inv = jnp.tile(inv, per_row).reshape(1, LANES)
    sign = jnp.repeat(jnp.array([-1.0, -1.0, 1.0, 1.0], F32), half).reshape(1, LANES)
    out_spec = pl.BlockSpec((tr * per_row, LANES), lambda i: (i, 0))
    cosp, sinp = pl.pallas_call(
        _rope_kernel,
        grid=(rows // tr,),
        in_specs=[pl.BlockSpec((tr, per_row), lambda i: (i, 0)),
                  pl.BlockSpec((1, LANES), lambda i: (0, 0)),
                  pl.BlockSpec((1, LANES), lambda i: (0, 0))],
        out_specs=[out_spec, out_spec],
        out_shape=[jax.ShapeDtypeStruct((b * s, LANES), F32)] * 2,
        compiler_params=_params(("parallel",)),
        name="rope_tables",
    )(positions.reshape(rows, per_row), inv, sign)
    return cosp.reshape(b, s, LANES), sinp.reshape(b, s, LANES)


def _log_sigmoid(x):
    return jnp.minimum(x, 0.0) - jnp.log1p(jnp.exp(-jnp.abs(x)))


def _inproj_kernel(x_ref, mod_ref, ng_ref, wqvo_ref, wkt_ref, wg_ref, wgt_ref, bcol_ref, brow_ref,
                   q_ref, kt_ref, v_ref, og_ref, gc_ref, gr_ref):
    m = mod_ref[0]
    xhat = _rms(x_ref[0])
    h = (xhat * (ng_ref[0:1] * (1.0 + m[1:2])) + m[0:1]).astype(BF16)
    nq = A_HEADS * A_DQK
    q = jnp.dot(h, wqvo_ref[:, :nq], preferred_element_type=F32)
    q_ref[0] = (q * (A_DQK ** -0.5)).astype(BF16)
    v = jnp.dot(h, wqvo_ref[:, nq:nq + D_MODEL], preferred_element_type=F32)
    v_ref[0] = v.astype(BF16)
    o = jnp.dot(h, wqvo_ref[:, nq + D_MODEL:], preferred_element_type=F32)
    og_ref[0] = (1.0 / (1.0 + jnp.exp(-o))).astype(BF16)
    nt = (((1,), (1,)), ((), ()))
    kt_ref[0] = lax.dot_general(wkt_ref[...], h, nt, preferred_element_type=F32).astype(BF16)
    gc = jnp.dot(h, wg_ref[...], preferred_element_type=F32) + bcol_ref[...]
    col = lax.broadcasted_iota(jnp.int32, gc.shape, 1)
    gc_ref[0] = jnp.where(col < A_HEADS, gc, _log_sigmoid(gc))
    gr = lax.dot_general(wgt_ref[...], h, nt, preferred_element_type=F32) + brow_ref[...]
    row = lax.broadcasted_iota(jnp.int32, gr.shape, 0)
    gr_ref[0] = jnp.where(row < A_HEADS, gr, _log_sigmoid(gr))


def _inproj(x, modp, ng, wqvo, wkt, wg, wgt, bcol, brow, tm=512):
    b, s, d = x.shape
    nq = A_HEADS * A_DQK
    ng8 = 2 * A_HEADS
    tok = lambda w: pl.BlockSpec((1, tm, w), lambda i, j: (i, j, 0))
    return pl.pallas_call(
        _inproj_kernel,
        grid=(b, s // tm),
        in_specs=[tok(d),
                  pl.BlockSpec((1, 8, d), lambda i, j: (i, 0, 0)),
                  _const_spec(ng.shape), _const_spec(wqvo.shape), _const_spec(wkt.shape),
                  _const_spec(wg.shape), _const_spec(wgt.shape), _const_spec(bcol.shape),
                  _const_spec(brow.shape)],
        out_specs=[tok(nq),
                   pl.BlockSpec((1, nq, tm), lambda i, j: (i, 0, j)),
                   tok(d), tok(d), tok(LANES),
                   pl.BlockSpec((1, ng8, tm), lambda i, j: (i, 0, j))],
        out_shape=[jax.ShapeDtypeStruct((b, s, nq), BF16),
                   jax.ShapeDtypeStruct((b, nq, s), BF16),
                   jax.ShapeDtypeStruct((b, s, d), BF16),
                   jax.ShapeDtypeStruct((b, s, d), BF16),
                   jax.ShapeDtypeStruct((b, s, LANES), F32),
                   jax.ShapeDtypeStruct((b, ng8, s), F32)],
        compiler_params=_params(("parallel", "parallel")),
        name="mlstm_inproj",
    )(x, modp, ng, wqvo, wkt, wg, wgt, bcol, brow)


def _mlstm_kernel(q_ref, kt_ref, v_ref, og_ref, gc_ref, gr_ref, hn_ref, o_ref, c_scr, n_scr, m_scr, *, L):
    @pl.when(pl.program_id(1) == 0)
    def _():
        c_scr[...] = jnp.zeros_like(c_scr)
        n_scr[...] = jnp.zeros_like(n_scr)
        m_scr[...] = jnp.zeros_like(m_scr)

    r = lax.broadcasted_iota(jnp.int32, (L, L), 0)
    cc = lax.broadcasted_iota(jnp.int32, (L, L), 1)
    causal = cc <= r
    tril = jnp.where(causal, 1.0, 0.0).astype(BF16)
    triu = jnp.where(r <= cc, 1.0, 0.0).astype(BF16)
    gc = gc_ref[0]
    gr = gr_ref[0]
    bcol_all = sum(jnp.dot(tril, t, preferred_element_type=F32) for t in _split3(gc))
    brow_all = sum(jnp.dot(t, triu, preferred_element_type=F32) for t in _split3(gr))
    nt = (((1,), (1,)), ((), ()))

    for hd in range(A_HEADS):
        fq = hd + A_HEADS
        i_row = gr[hd:hd + 1, :]
        b_row = brow_all[fq:fq + 1, :]
        i_col = gc[:, hd:hd + 1]
        b_col = bcol_all[:, fq:fq + 1]
        m_prev = m_scr[hd][0:1, 0:1]
        q = q_ref[0, :, hd * A_DQK:(hd + 1) * A_DQK]
        kt = kt_ref[0, hd * A_DQK:(hd + 1) * A_DQK, :]
        v = v_ref[0, :, hd * A_DV:(hd + 1) * A_DV]
        c_prev = c_scr[hd]
        n_prev = n_scr[hd][0:1, :]

        dmat = jnp.where(causal, b_col - b_row + i_row, -jnp.inf)
        inter = b_col + m_prev
        m_t = jnp.maximum(inter, jnp.max(dmat, axis=1, keepdims=True))
        w_inter = jnp.exp(inter - m_t)
        p = jnp.exp(dmat - m_t) * jnp.dot(q, kt, preferred_element_type=F32)
        num = (w_inter * jnp.dot(q, c_prev.astype(BF16), preferred_element_type=F32)
               + jnp.dot(p.astype(BF16), v, preferred_element_type=F32))
        den = (w_inter * jnp.sum(q.astype(F32) * n_prev, axis=1, keepdims=True)
               + jnp.sum(p, axis=1, keepdims=True))
        hh = num / jnp.maximum(jnp.abs(den), jnp.exp(-m_t))
        hh = _rms(hh) * hn_ref[hd:hd + 1, :]
        og = og_ref[0, :, hd * A_DV:(hd + 1) * A_DV].astype(F32)
        o_ref[0, :, hd * A_DV:(hd + 1) * A_DV] = (og * hh).astype(o_ref.dtype)

        b_last = b_row[:, L - 1:L]
        g_row = b_last - b_row + i_row
        m_new = jnp.maximum(b_last + m_prev, jnp.max(g_row, axis=1, keepdims=True))
        decay = jnp.exp(b_last + m_prev - m_new)
        wk = jnp.exp(g_row - m_new)
        kw = (kt.astype(F32) * wk).astype(BF16)
        c_scr[hd] = decay * c_prev + jnp.dot(kw, v, preferred_element_type=F32)
        wk8 = jnp.broadcast_to(wk, (8, L))
        n_add = sum(lax.dot_general(t, kt, nt, preferred_element_type=F32) for t in _split3(wk8))
        n_scr[hd] = decay * n_scr[hd] + n_add
        m_scr[hd] = jnp.broadcast_to(m_new, (8, LANES))


def _mlstm(q, kt, v, og, gc, gr, head_norm, L=256):
    b, s, d = v.shape
    nq = A_HEADS * A_DQK
    ng8 = 2 * A_HEADS
    tok = lambda w: pl.BlockSpec((1, L, w), lambda i, j: (i, j, 0))
    return pl.pallas_call(
        functools.partial(_mlstm_kernel, L=L),
        grid=(b, s // L),
        in_specs=[tok(nq),
                  pl.BlockSpec((1, nq, L), lambda i, j: (i, 0, j)),
                  tok(d), tok(d), tok(LANES),
                  pl.BlockSpec((1, ng8, L), lambda i, j: (i, 0, j)),
                  pl.BlockSpec(head_norm.shape, lambda i, j: (0, 0))],
        out_specs=tok(d),
        out_shape=jax.ShapeDtypeStruct((b, s, d), BF16),
        scratch_shapes=[pltpu.VMEM((A_HEADS, A_DQK, A_DV), F32),
                        pltpu.VMEM((A_HEADS, 8, A_DQK), F32),
                        pltpu.VMEM((A_HEADS, 8, LANES), F32)],
        compiler_params=_params(("parallel", "arbitrary")),
        name="mlstm_chunkwise",
    )(q, kt, v, og, gc, gr, head_norm)


def _post_kernel(a_ref, x_ref, mod_ref, ng_ref, wp_ref, w1_ref, w2_ref, o_ref, *, ff_chunk):
    m = mod_ref[0]
    y = jnp.dot(a_ref[0], wp_ref[...], preferred_element_type=F32)
    x1 = x_ref[0] + (m[2:3] * ng_ref[1:2]) * _rms(y)
    h = (_rms(x1) * (ng_ref[2:3] * (1.0 + m[4:5])) + m[3:4]).astype(BF16)
    acc = jnp.zeros_like(x1)
    for c in range(D_FF // ff_chunk):
        u = jnp.dot(h, w1_ref[:, c * ff_chunk:(c + 1) * ff_chunk], preferred_element_type=F32)
        u = jnp.square(jnp.maximum(u, 0.0)).astype(BF16)
        acc = acc + jnp.dot(u, w2_ref[c * ff_chunk:(c + 1) * ff_chunk, :], preferred_element_type=F32)
    o_ref[0] = x1 + (m[5:6] * ng_ref[3:4]) * _rms(acc)


def _post(a, x, modp, ng, wp, w1, w2, tm=512, ff_chunk=1024):
    b, s, d = x.shape
    tok = pl.BlockSpec((1, tm, d), lambda i, j: (i, j, 0))
    return pl.pallas_call(
        functools.partial(_post_kernel, ff_chunk=ff_chunk),
        grid=(b, s // tm),
        in_specs=[tok, tok,
                  pl.BlockSpec((1, 8, d), lambda i, j: (i, 0, 0)),
                  _const_spec(ng.shape), _const_spec(wp.shape), _const_spec(w1.shape),
                  _const_spec(w2.shape)],
        out_specs=tok,
        out_shape=jax.ShapeDtypeStruct((b, s, d), F32),
        compiler_params=_params(("parallel", "parallel")),
        name="outproj_mlp",
    )(a, x, modp, ng, wp, w1, w2)


def _rope_block(blk, cosp, sinp):
    return blk * cosp + pltpu.roll(blk, LANES // 2, axis=1) * sinp


def _qkv_kernel(x_ref, mod_ref, kvmod_ref, ng_ref, kvn_ref, wqa_ref, qn_ref, wqb_ref, wdkv_ref,
                kln_ref, wukv_ref, cos_ref, sin_ref, q_ref, k_ref, v_ref):
    m = mod_ref[0]
    km = kvmod_ref[0]
    xhat = _rms(x_ref[0])
    cosp = cos_ref[0]
    sinp = sin_ref[0]
    scale = (B_NOPE + B_ROPE) ** -0.5 * LOG2E

    h = (xhat * (ng_ref[0:1] * (1.0 + m[1:2])) + m[0:1]).astype(BF16)
    qa = jnp.dot(h, wqa_ref[...], preferred_element_type=F32)
    qn = (_rms(qa) * qn_ref[...]).astype(BF16)
    q = jnp.dot(qn, wqb_ref[...], preferred_element_type=F32) * scale
    for hd in range(B_HEADS):
        lo = hd * QK_PAD
        q_ref[0, :, lo:lo + B_NOPE] = q[:, lo:lo + B_NOPE].astype(BF16)
        q_ref[0, :, lo + B_NOPE:lo + QK_PAD] = _rope_block(
            q[:, lo + B_NOPE:lo + QK_PAD], cosp, sinp).astype(BF16)

    hs = (xhat * (kvn_ref[...] * (1.0 + km[1:2])) + km[0:1]).astype(BF16)
    dkv = jnp.dot(hs, wdkv_ref[...], preferred_element_type=F32)
    kr = _rope_block(dkv[:, B_KVLORA:], cosp, sinp).astype(BF16)
    ckv = (_rms(dkv[:, :B_KVLORA]) * kln_ref[...]).astype(BF16)
    kv = jnp.dot(ckv, wukv_ref[...], preferred_element_type=F32)
    for hd in range(B_HEADS):
        lo = hd * QK_PAD
        k_ref[0, :, lo:lo + B_NOPE] = kv[:, hd * B_NOPE:(hd + 1) * B_NOPE].astype(BF16)
        k_ref[0, :, lo + B_NOPE:lo + QK_PAD] = kr
    v_ref[0] = kv[:, B_HEADS * B_NOPE:].astype(BF16)


def _qkv(x, modp, kvmodp, ng, kvn, wqa, qnorm, wqb, wdkv, kln, wukv, cosp, sinp, tm=512):
    b, s, d = x.shape
    tok = lambda w: pl.BlockSpec((1, tm, w), lambda i, j: (i, j, 0))
    modspec = pl.BlockSpec((1, 8, d), lambda i, j: (i, 0, 0))
    hq = B_HEADS * QK_PAD
    return pl.pallas_call(
        _qkv_kernel,
        grid=(b, s // tm),
        in_specs=[tok(d), modspec, modspec,
                  _const_spec(ng.shape), _const_spec(kvn.shape), _const_spec(wqa.shape),
                  _const_spec(qnorm.shape), _const_spec(wqb.shape), _const_spec(wdkv.shape),
                  _const_spec(kln.shape), _const_spec(wukv.shape),
                  tok(LANES), tok(LANES)],
        out_specs=[tok(hq), tok(hq), tok(d)],
        out_shape=[jax.ShapeDtypeStruct((b, s, hq), BF16),
                   jax.ShapeDtypeStruct((b, s, hq), BF16),
                   jax.ShapeDtypeStruct((b, s, d), BF16)],
        compiler_params=_params(("parallel", "parallel")),
        name="mla_qkv",
    )(x, modp, kvmodp, ng, kvn, wqa, qnorm, wqb, wdkv, kln, wukv, cosp, sinp)


def _attn_kernel(q_ref, qnext_ref, k_ref, v_ref, o_ref, vaug_scr, s_scr, smax_scr, m_scr, acc_scr, *, tk):
    i = pl.program_id(2)
    s_len = v_ref.shape[1]
    tq = 2 * tk
    rows = 1024
    nt = (((1,), (1,)), ((), ()))
    lower = slice(0, tk)
    upper = slice(tk, tq)
    every = slice(0, tq)

    def scores(qsrc_ref, j, slot):
        off = pl.multiple_of(j * tk, tk)
        s = lax.dot_general(qsrc_ref[0], k_ref[0, pl.ds(off, tk), :], nt, preferred_element_type=F32)
        s_scr[slot] = s
        smax_scr[slot] = jnp.broadcast_to(jnp.max(s, axis=1, keepdims=True), (tq, LANES))

    @pl.when(i == 0)
    def _():
        ones_blk = jnp.ones((rows, LANES), BF16)

        def fill(r, carry):
            off = pl.multiple_of(r * rows, rows)
            vaug_scr[pl.ds(off, rows), 0:B_VDIM] = v_ref[0, pl.ds(off, rows), :]
            vaug_scr[pl.ds(off, rows), B_VDIM:] = ones_blk
            return carry
        lax.fori_loop(0, s_len // rows, fill, 0)
        scores(q_ref, 0, 0)
        scores(q_ref, 1, 1)

    m_scr[...] = jnp.full(m_scr.shape, -jnp.inf, F32)
    acc_scr[...] = jnp.zeros(acc_scr.shape, F32)

    def update(j, slot, rows_, diagonal):
        off = pl.multiple_of(j * tk, tk)
        s = s_scr[slot, rows_, :]
        if diagonal:
            qc = lax.broadcasted_iota(jnp.int32, s.shape, 0) // CHUNK
            kc = lax.broadcasted_iota(jnp.int32, s.shape, 1) // CHUNK
            s = jnp.where(kc <= qc, s, -jnp.inf)
            smax = jnp.broadcast_to(jnp.max(s, axis=1, keepdims=True), (s.shape[0], LANES))
        else:
            smax = smax_scr[slot, rows_, :]
        m = m_scr[rows_, :]
        m_new = jnp.maximum(m, smax)
        alpha = jnp.exp2(m - m_new)
        p = jnp.exp2(s - jnp.tile(m_new, (1, tk // LANES))).astype(BF16)
        pv = jnp.dot(p, vaug_scr[pl.ds(off, tk), :], preferred_element_type=F32)
        acc_scr[rows_, :] = jnp.tile(alpha, (1, 2 * B_VDIM // LANES)) * acc_scr[rows_, :] + pv
        m_scr[rows_, :] = m_new

    def pair(t, carry):
        j = 2 * t
        update(j, 0, every, False)
        scores(q_ref, j + 2, 0)
        update(j + 1, 1, every, False)
        scores(q_ref, j + 3, 1)
        return carry
    lax.fori_loop(0, i, pair, 0)

    update(2 * i, 0, lower, True)
    update(2 * i, 0, upper, False)
    scores(qnext_ref, 0, 0)
    update(2 * i + 1, 1, upper, True)
    scores(qnext_ref, 1, 1)
    o_ref[0] = (acc_scr[:, :B_VDIM] / acc_scr[:, B_VDIM:]).astype(o_ref.dtype)


def _attention(q, k, v, tk=512):
    b, s, _ = q.shape
    tq = 2 * tk
    nq = s // tq
    return pl.pallas_call(
        functools.partial(_attn_kernel, tk=tk),
        grid=(b, B_HEADS, nq),
        in_specs=[pl.BlockSpec((1, tq, QK_PAD), lambda bi, h, i: (bi, i, h)),
                  pl.BlockSpec((1, tq, QK_PAD), lambda bi, h, i: (bi, jnp.minimum(i + 1, nq - 1), h)),
                  pl.BlockSpec((1, s, QK_PAD), lambda bi, h, i: (bi, 0, h)),
                  pl.BlockSpec((1, s, B_VDIM), lambda bi, h, i: (bi, 0, h))],
        out_specs=pl.BlockSpec((1, tq, B_VDIM), lambda bi, h, i: (bi, i, h)),
        out_shape=jax.ShapeDtypeStruct((b, s, B_HEADS * B_VDIM), BF16),
        scratch_shapes=[pltpu.VMEM((s, 2 * B_VDIM), BF16),
                        pltpu.VMEM((2, tq, tk), F32),
                        pltpu.VMEM((2, tq, LANES), F32),
                        pltpu.VMEM((tq, LANES), F32),
                        pltpu.VMEM((tq, 2 * B_VDIM), F32)],
        compiler_params=_params(("parallel", "parallel", "arbitrary")),
        name="mla_attention",
    )(q, q, k, v)


def _pack_mod(vec6, b):
    m = vec6[:b].reshape(b, 6, D_MODEL)
    return jnp.pad(m, ((0, 0), (0, 2), (0, 0)))


def _rope_cols(w, n_lead):
    half = B_ROPE // 2
    z = jnp.zeros((w.shape[0], half), w.dtype)
    return jnp.concatenate([w[:, :n_lead], w[:, n_lead:n_lead + half], z, w[:, n_lead + half:], z], axis=1)


def kernel(x, c, positions, mod_w, mod_b, norm_g, ffn_w1, ffn_w2, a_w_in, a_b_gates, a_head_norm,
           a_w_out, b_w_qa, b_q_norm, b_w_qb, b_w_o, kv_mod_w, kv_mod_b, kv_norm, w_dkv,
           kv_lora_norm, w_ukv):
    b, s, d = x.shape

    c8 = jnp.pad(c, ((0, 8 - b), (0, 0)))
    mods = _mod_vectors(c8, mod_w, mod_b[:, None, :])
    kvmod = _mod_vectors(c8, kv_mod_w[None], kv_mod_b[None, None, :])[0]
    modp0 = _pack_mod(mods[0], b)
    modp1 = _pack_mod(mods[1], b)
    kvmodp = jnp.pad(kvmod[:b].reshape(b, 2, d), ((0, 0), (0, 6), (0, 0)))

    w_in = a_w_in[0]
    nq = A_HEADS * A_DQK
    wq, wk, wv, wo, wgate = (w_in[:, :nq], w_in[:, nq:2 * nq], w_in[:, 2 * nq:2 * nq + d],
                             w_in[:, 2 * nq + d:2 * nq + 2 * d], w_in[:, 2 * nq + 2 * d:])
    wqvo = jnp.concatenate([wq, wv, wo], axis=1).astype(BF16)
    wkt = wk.T.astype(BF16)
    ng8 = 2 * A_HEADS
    wg = jnp.pad(wgate, ((0, 0), (0, LANES - ng8))).astype(BF16)
    wgt = wgate.T.astype(BF16)
    bcol = jnp.pad(a_b_gates[0], (0, LANES - ng8)).reshape(1, LANES)
    brow = a_b_gates[0].reshape(ng8, 1)
    q, kt, v, og, gc, gr = _inproj(x, modp0, norm_g[0], wqvo, wkt, wg, wgt, bcol, brow)
    a = _mlstm(q, kt, v, og, gc, gr, a_head_norm[0])
    x = _post(a, x, modp0, norm_g[0], a_w_out[0].astype(BF16), ffn_w1[0].astype(BF16),
              ffn_w2[0].astype(BF16))

    cosp, sinp = _rope_tables(positions)
    wqb = b_w_qb[0].reshape(B_QLORA * B_HEADS, B_NOPE + B_ROPE)
    wqb = _rope_cols(wqb, B_NOPE).reshape(B_QLORA, B_HEADS * QK_PAD).astype(BF16)
    wdkv = _rope_cols(w_dkv, B_KVLORA).astype(BF16)
    wukv = w_ukv.reshape(B_KVLORA, B_HEADS, B_NOPE + B_VDIM)
    wukv = jnp.concatenate([wukv[:, :, :B_NOPE].reshape(B_KVLORA, -1),
                            wukv[:, :, B_NOPE:].reshape(B_KVLORA, -1)], axis=1).astype(BF16)
    qh, kh, vh = _qkv(x, modp1, kvmodp, norm_g[1], kv_norm.reshape(1, d), b_w_qa[0].astype(BF16),
                      b_q_norm[0].reshape(1, -1), wqb, wdkv, kv_lora_norm.reshape(1, -1), wukv,
                      cosp, sinp)
    a = _attention(qh, kh, vh)
    x = _post(a, x, modp1, norm_g[1], b_w_o[0].astype(BF16), ffn_w1[1].astype(BF16),
              ffn_w2[1].astype(BF16))
    return x
```

```python
import functools

import jax
import jax.numpy as jnp
from jax import lax
from jax.experimental import pallas as pl
from jax.experimental.pallas import tpu as pltpu

D_MODEL = 1024
CHUNK = 64
EPS = 1e-6
D_FF = 4 * D_MODEL
A_HEADS = 4
A_DQK = 128
A_DV = D_MODEL // A_HEADS
B_HEADS = 8
B_NOPE = 128
B_ROPE = 64
B_VDIM = D_MODEL // B_HEADS
B_QLORA = 384
B_KVLORA = 256
ROPE_THETA = 10000.0

LANES = 128
QK_PAD = 256
VMEM_LIMIT = 56 * 1024 * 1024
LOG2E = 1.4426950408889634

F32 = jnp.float32
BF16 = jnp.bfloat16


def _params(sem, vmem=VMEM_LIMIT):
    return pltpu.CompilerParams(dimension_semantics=sem, vmem_limit_bytes=vmem)


def _const_spec(shape):
    nd = len(shape)
    return pl.BlockSpec(shape, lambda *_: (0,) * nd, pipeline_mode=pl.Buffered(1))


def _rms(x):
    return x * lax.rsqrt(jnp.mean(x * x, axis=-1, keepdims=True) + EPS)


def _split3(x):
    hi = x.astype(BF16)
    r = x - hi.astype(F32)
    mid = r.astype(BF16)
    lo = (r - mid.astype(F32)).astype(BF16)
    return hi, mid, lo


def _mod_kernel(c_ref, w_ref, b_ref, o_ref):
    c = c_ref[...]
    c_act = c * (1.0 / (1.0 + jnp.exp(-c)))
    w = w_ref[0].astype(BF16)
    o_ref[0] = jnp.dot(c_act.astype(BF16), w, preferred_element_type=F32) + b_ref[0]


def _mod_vectors(c8, w, b, tn=1024):
    g, d, n = w.shape
    return pl.pallas_call(
        _mod_kernel,
        grid=(g, n // tn),
        in_specs=[pl.BlockSpec((8, d), lambda i, j: (0, 0)),
                  pl.BlockSpec((1, d, tn), lambda i, j: (i, 0, j)),
                  pl.BlockSpec((1, 1, tn), lambda i, j: (i, 0, j))],
        out_specs=pl.BlockSpec((1, 8, tn), lambda i, j: (i, 0, j)),
        out_shape=jax.ShapeDtypeStruct((g, 8, n), F32),
        compiler_params=_params(("parallel", "parallel")),
        name="mod_vectors",
    )(c8, w, b)


def _rope_kernel(pos_ref, inv_ref, sign_ref, cos_ref, sin_ref):
    half = B_ROPE // 2
    per_row = LANES // half
    tr = pos_ref.shape[0]
    pos = pos_ref[...].astype(F32)
    lane_tok = lax.broadcasted_iota(jnp.int32, (tr, LANES), 1) // half
    pos_d = jnp.zeros((tr, LANES), F32)
    for j in range(per_row):
        pos_d = jnp.where(lane_tok == j, pos[:, j:j + 1], pos_d)
    ang = pos_d * inv_ref[...]
    c = jnp.cos(ang)
    sn = jnp.sin(ang)
    for j in range(per_row):
        cj = c[:, j * half:(j + 1) * half]
        sj = sn[:, j * half:(j + 1) * half]
        cos_ref[pl.ds(j, tr, stride=per_row), :] = jnp.concatenate([cj] * per_row, axis=1)
        sin_ref[pl.ds(j, tr, stride=per_row), :] = jnp.concatenate([sj] * per_row, axis=1) * sign_ref[...]


def _rope_tables(positions, tr=512):
    b, s = positions.shape
    half = B_ROPE // 2
    per_row = LANES // half
    rows = b * s // per_row
    inv = ROPE_THETA ** (-jnp.arange(0, B_ROPE, 2, dtype=F32) / B_ROPE)
    inv = jnp.tile(inv, per_row).reshape(1, LANES)
    sign = jnp.repeat(jnp.array([-1.0, -1.0, 1.0, 1.0], F32), half).reshape(1, LANES)
    out_spec = pl.BlockSpec((tr * per_row, LANES), lambda i: (i, 0))
    cosp, sinp = pl.pallas_call(
        _rope_kernel,
        grid=(rows // tr,),
        in_specs=[pl.BlockSpec((tr, per_row), lambda i: (i, 0)),
                  pl.BlockSpec((1, LANES), lambda i: (0, 0)),
                  pl.BlockSpec((1, LANES), lambda i: (0, 0))],
        out_specs=[out_spec, out_spec],
        out_shape=[jax.ShapeDtypeStruct((b * s, LANES), F32)] * 2,
        compiler_params=_params(("parallel",)),
        name="rope_tables",
    )(positions.reshape(rows, per_row), inv, sign)
    return cosp.reshape(b, s, LANES), sinp.reshape(b, s, LANES)


def _log_sigmoid(x):
    return jnp.minimum(x, 0.0) - jnp.log1p(jnp.exp(-jnp.abs(x)))


def _inproj_kernel(x_ref, mod_ref, ng_ref, wqvo_ref, wkt_ref, wg_ref, wgt_ref, bcol_ref, brow_ref,
                   q_ref, kt_ref, v_ref, og_ref, gc_ref, gr_ref):
    m = mod_ref[0]
    xhat = _rms(x_ref[0])
    h = (xhat * (ng_ref[0:1] * (1.0 + m[1:2])) + m[0:1]).astype(BF16)
    nq = A_HEADS * A_DQK
    q = jnp.dot(h, wqvo_ref[:, :nq], preferred_element_type=F32)
    q_ref[0] = (q * (A_DQK ** -0.5)).astype(BF16)
    v = jnp.dot(h, wqvo_ref[:, nq:nq + D_MODEL], preferred_element_type=F32)
    v_ref[0] = v.astype(BF16)
    o = jnp.dot(h, wqvo_ref[:, nq + D_MODEL:], preferred_element_type=F32)
    og_ref[0] = (1.0 / (1.0 + jnp.exp(-o))).astype(BF16)
    nt = (((1,), (1,)), ((), ()))
    kt_ref[0] = lax.dot_general(wkt_ref[...], h, nt, preferred_element_type=F32).astype(BF16)
    gc = jnp.dot(h, wg_ref[...], preferred_element_type=F32) + bcol_ref[...]
    col = lax.broadcasted_iota(jnp.int32, gc.shape, 1)
    gc_ref[0] = jnp.where(col < A_HEADS, gc, _log_sigmoid(gc))
    gr = lax.dot_general(wgt_ref[...], h, nt, preferred_element_type=F32) + brow_ref[...]
    row = lax.broadcasted_iota(jnp.int32, gr.shape, 0)
    gr_ref[0] = jnp.where(row < A_HEADS, gr, _log_sigmoid(gr))


def _inproj(x, modp, ng, wqvo, wkt, wg, wgt, bcol, brow, tm=1024):
    b, s, d = x.shape
    nq = A_HEADS * A_DQK
    ng8 = 2 * A_HEADS
    tok = lambda w: pl.BlockSpec((1, tm, w), lambda i, j: (i, j, 0))
    return pl.pallas_call(
        _inproj_kernel,
        grid=(b, s // tm),
        in_specs=[tok(d),
                  pl.BlockSpec((1, 8, d), lambda i, j: (i, 0, 0)),
                  _const_spec(ng.shape), _const_spec(wqvo.shape), _const_spec(wkt.shape),
                  _const_spec(wg.shape), _const_spec(wgt.shape), _const_spec(bcol.shape),
                  _const_spec(brow.shape)],
        out_specs=[tok(nq),
                   pl.BlockSpec((1, nq, tm), lambda i, j: (i, 0, j)),
                   tok(d), tok(d), tok(LANES),
                   pl.BlockSpec((1, ng8, tm), lambda i, j: (i, 0, j))],
        out_shape=[jax.ShapeDtypeStruct((b, s, nq), BF16),
                   jax.ShapeDtypeStruct((b, nq, s), BF16),
                   jax.ShapeDtypeStruct((b, s, d), BF16),
                   jax.ShapeDtypeStruct((b, s, d), BF16),
                   jax.ShapeDtypeStruct((b, s, LANES), F32),
                   jax.ShapeDtypeStruct((b, ng8, s), F32)],
        compiler_params=_params(("parallel", "parallel")),
        name="mlstm_inproj",
    )(x, modp, ng, wqvo, wkt, wg, wgt, bcol, brow)


def _mlstm_kernel(q_ref, kt_ref, v_ref, og_ref, gc_ref, gr_ref, hn_ref, o_ref, c_scr, n_scr, m_scr, *, L):
    @pl.when(pl.program_id(1) == 0)
    def _():
        c_scr[...] = jnp.zeros_like(c_scr)
        n_scr[...] = jnp.zeros_like(n_scr)
        m_scr[...] = jnp.zeros_like(m_scr)

    r = lax.broadcasted_iota(jnp.int32, (L, L), 0)
    cc = lax.broadcasted_iota(jnp.int32, (L, L), 1)
    causal = cc <= r
    tril = jnp.where(causal, 1.0, 0.0).astype(BF16)
    triu = jnp.where(r <= cc, 1.0, 0.0).astype(BF16)
    gc = gc_ref[0]
    gr = gr_ref[0]
    bcol_all = sum(jnp.dot(tril, t, preferred_element_type=F32) for t in _split3(gc))
    brow_all = sum(jnp.dot(t, triu, preferred_element_type=F32) for t in _split3(gr))
    nt = (((1,), (1,)), ((), ()))

    for hd in range(A_HEADS):
        fq = hd + A_HEADS
        i_row = gr[hd:hd + 1, :]
        b_row = brow_all[fq:fq + 1, :]
        i_col = gc[:, hd:hd + 1]
        b_col = bcol_all[:, fq:fq + 1]
        m_prev = m_scr[hd][0:1, 0:1]
        q = q_ref[0, :, hd * A_DQK:(hd + 1) * A_DQK]
        kt = kt_ref[0, hd * A_DQK:(hd + 1) * A_DQK, :]
        v = v_ref[0, :, hd * A_DV:(hd + 1) * A_DV]
        c_prev = c_scr[hd]
        n_prev = n_scr[hd][0:1, :]

        dmat = jnp.where(causal, b_col - b_row + i_row, -jnp.inf)
        inter = b_col + m_prev
        m_t = jnp.maximum(inter, jnp.max(dmat, axis=1, keepdims=True))
        w_inter = jnp.exp(inter - m_t)
        p = jnp.exp(dmat - m_t) * jnp.dot(q, kt, preferred_element_type=F32)
        num = (w_inter * jnp.dot(q, c_prev.astype(BF16), preferred_element_type=F32)
               + jnp.dot(p.astype(BF16), v, preferred_element_type=F32))
        den = (w_inter * jnp.sum(q.astype(F32) * n_prev, axis=1, keepdims=True)
               + jnp.sum(p, axis=1, keepdims=True))
        hh = num / jnp.maximum(jnp.abs(den), jnp.exp(-m_t))
        hh = _rms(hh) * hn_ref[hd:hd + 1, :]
        og = og_ref[0, :, hd * A_DV:(hd + 1) * A_DV].astype(F32)
        o_ref[0, :, hd * A_DV:(hd + 1) * A_DV] = (og * hh).astype(o_ref.dtype)

        b_last = b_row[:, L - 1:L]
        g_row = b_last - b_row + i_row
        m_new = jnp.maximum(b_last + m_prev, jnp.max(g_row, axis=1, keepdims=True))
        decay = jnp.exp(b_last + m_prev - m_new)
        wk = jnp.exp(g_row - m_new)
        kw = (kt.astype(F32) * wk).astype(BF16)
        c_scr[hd] = decay * c_prev + jnp.dot(kw, v, preferred_element_type=F32)
        wk8 = jnp.broadcast_to(wk, (8, L))
        n_add = sum(lax.dot_general(t, kt, nt, preferred_element_type=F32) for t in _split3(wk8))
        n_scr[hd] = decay * n_scr[hd] + n_add
        m_scr[hd] = jnp.broadcast_to(m_new, (8, LANES))


def _mlstm(q, kt, v, og, gc, gr, head_norm, L=256):
    b, s, d = v.shape
    nq = A_HEADS * A_DQK
    ng8 = 2 * A_HEADS
    tok = lambda w: pl.BlockSpec((1, L, w), lambda i, j: (i, j, 0))
    return pl.pallas_call(
        functools.partial(_mlstm_kernel, L=L),
        grid=(b, s // L),
        in_specs=[tok(nq),
                  pl.BlockSpec((1, nq, L), lambda i, j: (i, 0, j)),
                  tok(d), tok(d), tok(LANES),
                  pl.BlockSpec((1, ng8, L), lambda i, j: (i, 0, j)),
                  pl.BlockSpec(head_norm.shape, lambda i, j: (0, 0))],
        out_specs=tok(d),
        out_shape=jax.ShapeDtypeStruct((b, s, d), BF16),
        scratch_shapes=[pltpu.VMEM((A_HEADS, A_DQK, A_DV), F32),
                        pltpu.VMEM((A_HEADS, 8, A_DQK), F32),
                        pltpu.VMEM((A_HEADS, 8, LANES), F32)],
        compiler_params=_params(("parallel", "arbitrary")),
        name="mlstm_chunkwise",
    )(q, kt, v, og, gc, gr, head_norm)


def _post_kernel(a_ref, x_ref, mod_ref, ng_ref, wp_ref, w1_ref, w2_ref, o_ref, *, ff_chunk):
    m = mod_ref[0]
    y = jnp.dot(a_ref[0], wp_ref[...], preferred_element_type=F32)
    x1 = x_ref[0] + (m[2:3] * ng_ref[1:2]) * _rms(y)
    h = (_rms(x1) * (ng_ref[2:3] * (1.0 + m[4:5])) + m[3:4]).astype(BF16)
    acc = jnp.zeros_like(x1)
    for c in range(D_FF // ff_chunk):
        u = jnp.dot(h, w1_ref[:, c * ff_chunk:(c + 1) * ff_chunk], preferred_element_type=F32)
        u = jnp.square(jnp.maximum(u, 0.0)).astype(BF16)
        acc = acc + jnp.dot(u, w2_ref[c * ff_chunk:(c + 1) * ff_chunk, :], preferred_element_type=F32)
    o_ref[0] = x1 + (m[5:6] * ng_ref[3:4]) * _rms(acc)


def _post(a, x, modp, ng, wp, w1, w2, tm=512, ff_chunk=1024):
    b, s, d = x.shape
    tok = pl.BlockSpec((1, tm, d), lambda i, j: (i, j, 0))
    return pl.pallas_call(
        functools.partial(_post_kernel, ff_chunk=ff_chunk),
        grid=(b, s // tm),
        in_specs=[tok, tok,
                  pl.BlockSpec((1, 8, d), lambda i, j: (i, 0, 0)),
                  _const_spec(ng.shape), _const_spec(wp.shape), _const_spec(w1.shape),
                  _const_spec(w2.shape)],
        out_specs=tok,
        out_shape=jax.ShapeDtypeStruct((b, s, d), F32),
        compiler_params=_params(("parallel", "parallel")),
        name="outproj_mlp",
    )(a, x, modp, ng, wp, w1, w2)


def _rope_block(blk, cosp, sinp):
    return blk * cosp + pltpu.roll(blk, LANES // 2, axis=1) * sinp


def _qkv_kernel(x_ref, mod_ref, kvmod_ref, ng_ref, kvn_ref, wqa_ref, qn_ref, wqb_ref, wdkv_ref,
                kln_ref, wukv_ref, cos_ref, sin_ref, q_ref, k_ref, v_ref):
    m = mod_ref[0]
    km = kvmod_ref[0]
    xhat = _rms(x_ref[0])
    cosp = cos_ref[0]
    sinp = sin_ref[0]
    scale = (B_NOPE + B_ROPE) ** -0.5 * LOG2E

    h = (xhat * (ng_ref[0:1] * (1.0 + m[1:2])) + m[0:1]).astype(BF16)
    qa = jnp.dot(h, wqa_ref[...], preferred_element_type=F32)
    qn = (_rms(qa) * qn_ref[...]).astype(BF16)
    q = jnp.dot(qn, wqb_ref[...], preferred_element_type=F32) * scale
    for hd in range(B_HEADS):
        lo = hd * QK_PAD
        q_ref[0, :, lo:lo + B_NOPE] = q[:, lo:lo + B_NOPE].astype(BF16)
        q_ref[0, :, lo + B_NOPE:lo + QK_PAD] = _rope_block(
            q[:, lo + B_NOPE:lo + QK_PAD], cosp, sinp).astype(BF16)

    hs = (xhat * (kvn_ref[...] * (1.0 + km[1:2])) + km[0:1]).astype(BF16)
    dkv = jnp.dot(hs, wdkv_ref[...], preferred_element_type=F32)
    kr = _rope_block(dkv[:, B_KVLORA:], cosp, sinp).astype(BF16)
    ckv = (_rms(dkv[:, :B_KVLORA]) * kln_ref[...]).astype(BF16)
    kv = jnp.dot(ckv, wukv_ref[...], preferred_element_type=F32)
    for hd in range(B_HEADS):
        lo = hd * QK_PAD
        k_ref[0, :, lo:lo + B_NOPE] = kv[:, hd * B_NOPE:(hd + 1) * B_NOPE].astype(BF16)
        k_ref[0, :, lo + B_NOPE:lo + QK_PAD] = kr
    v_ref[0] = kv[:, B_HEADS * B_NOPE:].astype(BF16)


def _qkv(x, modp, kvmodp, ng, kvn, wqa, qnorm, wqb, wdkv, kln, wukv, cosp, sinp, tm=1024):
    b, s, d = x.shape
    tok = lambda w: pl.BlockSpec((1, tm, w), lambda i, j: (i, j, 0))
    modspec = pl.BlockSpec((1, 8, d), lambda i, j: (i, 0, 0))
    hq = B_HEADS * QK_PAD
    return pl.pallas_call(
        _qkv_kernel,
        grid=(b, s // tm),
        in_specs=[tok(d), modspec, modspec,
                  _const_spec(ng.shape), _const_spec(kvn.shape), _const_spec(wqa.shape),
                  _const_spec(qnorm.shape), _const_spec(wqb.shape), _const_spec(wdkv.shape),
                  _const_spec(kln.shape), _const_spec(wukv.shape),
                  tok(LANES), tok(LANES)],
        out_specs=[tok(hq), tok(hq), tok(d)],
        out_shape=[jax.ShapeDtypeStruct((b, s, hq), BF16),
                   jax.ShapeDtypeStruct((b, s, hq), BF16),
                   jax.ShapeDtypeStruct((b, s, d), BF16)],
        compiler_params=_params(("parallel", "parallel")),
        name="mla_qkv",
    )(x, modp, kvmodp, ng, kvn, wqa, qnorm, wqb, wdkv, kln, wukv, cosp, sinp)


def _attn_kernel(q_ref, qnext_ref, k_ref, v_ref, o_ref, vaug_scr, s_scr, smax_scr, m_scr, acc_scr, *, tk):
    i = pl.program_id(2)
    s_len = v_ref.shape[1]
    tq = 2 * tk
    rows = 1024
    nt = (((1,), (1,)), ((), ()))
    lower = slice(0, tk)
    upper = slice(tk, tq)
    every = slice(0, tq)

    def scores(qsrc_ref, j, slot):
        off = pl.multiple_of(j * tk, tk)
        s = lax.dot_general(qsrc_ref[0], k_ref[0, pl.ds(off, tk), :], nt, preferred_element_type=F32)
        s_scr[slot] = s
        smax_scr[slot] = jnp.broadcast_to(jnp.max(s, axis=1, keepdims=True), (tq, LANES))

    @pl.when(i == 0)
    def _():
        ones_blk = jnp.ones((rows, LANES), BF16)

        def fill(r, carry):
            off = pl.multiple_of(r * rows, rows)
            vaug_scr[pl.ds(off, rows), 0:B_VDIM] = v_ref[0, pl.ds(off, rows), :]
            vaug_scr[pl.ds(off, rows), B_VDIM:] = ones_blk
            return carry
        lax.fori_loop(0, s_len // rows, fill, 0)
        scores(q_ref, 0, 0)
        scores(q_ref, 1, 1)

    m_scr[...] = jnp.full(m_scr.shape, -jnp.inf, F32)
    acc_scr[...] = jnp.zeros(acc_scr.shape, F32)

    def update(j, slot, rows_, diagonal):
        off = pl.multiple_of(j * tk, tk)
        s = s_scr[slot, rows_, :]
        if diagonal:
            qc = lax.broadcasted_iota(jnp.int32, s.shape, 0) // CHUNK
            kc = lax.broadcasted_iota(jnp.int32, s.shape, 1) // CHUNK
            s = jnp.where(kc <= qc, s, -jnp.inf)
            smax = jnp.broadcast_to(jnp.max(s, axis=1, keepdims=True), (s.shape[0], LANES))
        else:
            smax = smax_scr[slot, rows_, :]
        m = m_scr[rows_, :]
        m_new = jnp.maximum(m, smax)
        alpha = jnp.exp2(m - m_new)
        p = jnp.exp2(s - jnp.tile(m_new, (1, tk // LANES))).astype(BF16)
        pv = jnp.dot(p, vaug_scr[pl.ds(off, tk), :], preferred_element_type=F32)
        acc_scr[rows_, :] = jnp.tile(alpha, (1, 2 * B_VDIM // LANES)) * acc_scr[rows_, :] + pv
        m_scr[rows_, :] = m_new

    def pair(j):
        update(j, 0, every, False)
        scores(q_ref, j + 2, 0)
        update(j + 1, 1, every, False)
        scores(q_ref, j + 3, 1)

    def two_pairs(t, carry):
        pair(4 * t)
        pair(4 * t + 2)
        return carry
    lax.fori_loop(0, i // 2, two_pairs, 0)

    @pl.when(i % 2 == 1)
    def _():
        pair(2 * i - 2)

    update(2 * i, 0, lower, True)
    update(2 * i, 0, upper, False)
    scores(qnext_ref, 0, 0)
    update(2 * i + 1, 1, upper, True)
    scores(qnext_ref, 1, 1)
    o_ref[0] = (acc_scr[:, :B_VDIM] / acc_scr[:, B_VDIM:]).astype(o_ref.dtype)


def _attention(q, k, v, tk=512):
    b, s, _ = q.shape
    tq = 2 * tk
    nq = s // tq
    return pl.pallas_call(
        functools.partial(_attn_kernel, tk=tk),
        grid=(b, B_HEADS, nq),
        in_specs=[pl.BlockSpec((1, tq, QK_PAD), lambda bi, h, i: (bi, i, h)),
                  pl.BlockSpec((1, tq, QK_PAD), lambda bi, h, i: (bi, jnp.minimum(i + 1, nq - 1), h)),
                  pl.BlockSpec((1, s, QK_PAD), lambda bi, h, i: (bi, 0, h)),
                  pl.BlockSpec((1, s, B_VDIM), lambda bi, h, i: (bi, 0, h))],
        out_specs=pl.BlockSpec((1, tq, B_VDIM), lambda bi, h, i: (bi, i, h)),
        out_shape=jax.ShapeDtypeStruct((b, s, B_HEADS * B_VDIM), BF16),
        scratch_shapes=[pltpu.VMEM((s, 2 * B_VDIM), BF16),
                        pltpu.VMEM((2, tq, tk), F32),
                        pltpu.VMEM((2, tq, LANES), F32),
                        pltpu.VMEM((tq, LANES), F32),
                        pltpu.VMEM((tq, 2 * B_VDIM), F32)],
        compiler_params=_params(("parallel", "parallel", "arbitrary")),
        name="mla_attention",
    )(q, q, k, v)


def _pack_mod(vec6, b):
    m = vec6[:b].reshape(b, 6, D_MODEL)
    return jnp.pad(m, ((0, 0), (0, 2), (0, 0)))


def _rope_cols(w, n_lead):
    half = B_ROPE // 2
    z = jnp.zeros((w.shape[0], half), w.dtype)
    return jnp.concatenate([w[:, :n_lead], w[:, n_lead:n_lead + half], z, w[:, n_lead + half:], z], axis=1)


def kernel(x, c, positions, mod_w, mod_b, norm_g, ffn_w1, ffn_w2, a_w_in, a_b_gates, a_head_norm,
           a_w_out, b_w_qa, b_q_norm, b_w_qb, b_w_o, kv_mod_w, kv_mod_b, kv_norm, w_dkv,
           kv_lora_norm, w_ukv):
    b, s, d = x.shape

    c8 = jnp.pad(c, ((0, 8 - b), (0, 0)))
    mods = _mod_vectors(c8, mod_w, mod_b[:, None, :])
    kvmod = _mod_vectors(c8, kv_mod_w[None], kv_mod_b[None, None, :])[0]
    modp0 = _pack_mod(mods[0], b)
    modp1 = _pack_mod(mods[1], b)
    kvmodp = jnp.pad(kvmod[:b].reshape(b, 2, d), ((0, 0), (0, 6), (0, 0)))

    w_in = a_w_in[0]
    nq = A_HEADS * A_DQK
    wq, wk, wv, wo, wgate = (w_in[:, :nq], w_in[:, nq:2 * nq], w_in[:, 2 * nq:2 * nq + d],
                             w_in[:, 2 * nq + d:2 * nq + 2 * d], w_in[:, 2 * nq + 2 * d:])
    wqvo = jnp.concatenate([wq, wv, wo], axis=1).astype(BF16)
    wkt = wk.T.astype(BF16)
    ng8 = 2 * A_HEADS
    wg = jnp.pad(wgate, ((0, 0), (0, LANES - ng8))).astype(BF16)
    wgt = wgate.T.astype(BF16)
    bcol = jnp.pad(a_b_gates[0], (0, LANES - ng8)).reshape(1, LANES)
    brow = a_b_gates[0].reshape(ng8, 1)
    q, kt, v, og, gc, gr = _inproj(x, modp0, norm_g[0], wqvo, wkt, wg, wgt, bcol, brow)
    a = _mlstm(q, kt, v, og, gc, gr, a_head_norm[0])
    x = _post(a, x, modp0, norm_g[0], a_w_out[0].astype(BF16), ffn_w1[0].astype(BF16),
              ffn_w2[0].astype(BF16))

    cosp, sinp = _rope_tables(positions)
    wqb = b_w_qb[0].reshape(B_QLORA * B_HEADS, B_NOPE + B_ROPE)
    wqb = _rope_cols(wqb, B_NOPE).reshape(B_QLORA, B_HEADS * QK_PAD).astype(BF16)
    wdkv = _rope_cols(w_dkv, B_KVLORA).astype(BF16)
    wukv = w_ukv.reshape(B_KVLORA, B_HEADS, B_NOPE + B_VDIM)
    wukv = jnp.concatenate([wukv[:, :, :B_NOPE].reshape(B_KVLORA, -1),
                            wukv[:, :, B_NOPE:].reshape(B_KVLORA, -1)], axis=1).astype(BF16)
    qh, kh, vh = _qkv(x, modp1, kvmodp, norm_g[1], kv_norm.reshape(1, d), b_w_qa[0].astype(BF16),
                      b_q_norm[0].reshape(1, -1), wqb, wdkv, kv_lora_norm.reshape(1, -1), wukv,
                      cosp, sinp)
    a = _attention(qh, kh, vh)
    x = _post(a, x, modp1, norm_g[1], b_w_o[0].astype(BF16), ffn_w1[1].astype(BF16),
              ffn_w2[1].astype(BF16))
    return x
```

```python
import functools

import jax
import jax.numpy as jnp
from jax import lax
from jax.experimental import pallas as pl
from jax.experimental.pallas import tpu as pltpu

D_MODEL = 1024
CHUNK = 64
EPS = 1e-6
D_FF = 4 * D_MODEL
A_HEADS = 4
A_DQK = 128
A_DV = D_MODEL // A_HEADS
B_HEADS = 8
B_NOPE = 128
B_ROPE = 64
B_VDIM = D_MODEL // B_HEADS
B_QLORA = 384
B_KVLORA = 256
ROPE_THETA = 10000.0

LANES = 128
QK_PAD = 256
VMEM_LIMIT = 60 * 1024 * 1024
LOG2E = 1.4426950408889634

F32 = jnp.float32
BF16 = jnp.bfloat16


def _params(sem, vmem=VMEM_LIMIT):
    return pltpu.CompilerParams(dimension_semantics=sem, vmem_limit_bytes=vmem)


def _const_spec(shape):
    nd = len(shape)
    return pl.BlockSpec(shape, lambda *_: (0,) * nd, pipeline_mode=pl.Buffered(1))


def _rms(x):
    return x * lax.rsqrt(jnp.mean(x * x, axis=-1, keepdims=True) + EPS)


def _split3(x):
    hi = x.astype(BF16)
    r = x - hi.astype(F32)
    mid = r.astype(BF16)
    lo = (r - mid.astype(F32)).astype(BF16)
    return hi, mid, lo


def _mod_kernel(c_ref, w_ref, b_ref, o_ref):
    c = c_ref[...]
    c_act = c * (1.0 / (1.0 + jnp.exp(-c)))
    w = w_ref[0].astype(BF16)
    o_ref[0] = jnp.dot(c_act.astype(BF16), w, preferred_element_type=F32) + b_ref[0]


def _mod_vectors(c8, w, b, tn=1024):
    g, d, n = w.shape
    return pl.pallas_call(
        _mod_kernel,
        grid=(g, n // tn),
        in_specs=[pl.BlockSpec((8, d), lambda i, j: (0, 0)),
                  pl.BlockSpec((1, d, tn), lambda i, j: (i, 0, j)),
                  pl.BlockSpec((1, 1, tn), lambda i, j: (i, 0, j))],
        out_specs=pl.BlockSpec((1, 8, tn), lambda i, j: (i, 0, j)),
        out_shape=jax.ShapeDtypeStruct((g, 8, n), F32),
        compiler_params=_params(("parallel", "parallel")),
        name="mod_vectors",
    )(c8, w, b)


def _rope_kernel(pos_ref, inv_ref, sign_ref, cos_ref, sin_ref):
    half = B_ROPE // 2
    per_row = LANES // half
    tr = pos_ref.shape[0]
    pos = pos_ref[...].astype(F32)
    lane_tok = lax.broadcasted_iota(jnp.int32, (tr, LANES), 1) // half
    pos_d = jnp.zeros((tr, LANES), F32)
    for j in range(per_row):
        pos_d = jnp.where(lane_tok == j, pos[:, j:j + 1], pos_d)
    ang = pos_d * inv_ref[...]
    c = jnp.cos(ang)
    sn = jnp.sin(ang)
    for j in range(per_row):
        cj = c[:, j * half:(j + 1) * half]
        sj = sn[:, j * half:(j + 1) * half]
        cos_ref[pl.ds(j, tr, stride=per_row), :] = jnp.concatenate([cj] * per_row, axis=1)
        sin_ref[pl.ds(j, tr, stride=per_row), :] = jnp.concatenate([sj] * per_row, axis=1) * sign_ref[...]


def _rope_tables(positions, tr=512):
    b, s = positions.shape
    half = B_ROPE // 2
    per_row = LANES // half
    rows = b * s // per_row
    inv = ROPE_THETA ** (-jnp.arange(0, B_ROPE, 2, dtype=F32) / B_ROPE)
    inv = jnp.tile(inv, per_row).reshape(1, LANES)
    sign = jnp.repeat(jnp.array([-1.0, -1.0, 1.0, 1.0], F32), half).reshape(1, LANES)
    out_spec = pl.BlockSpec((tr * per_row, LANES), lambda i: (i, 0))
    cosp, sinp = pl.pallas_call(
        _rope_kernel,
        grid=(rows // tr,),
        in_specs=[pl.BlockSpec((tr, per_row), lambda i: (i, 0)),
                  pl.BlockSpec((1, LANES), lambda i: (0, 0)),
                  pl.BlockSpec((1, LANES), lambda i: (0, 0))],
        out_specs=[out_spec, out_spec],
        out_shape=[jax.ShapeDtypeStruct((b * s, LANES), F32)] * 2,
        compiler_params=_params(("parallel",)),
        name="rope_tables",
    )(positions.reshape(rows, per_row), inv, sign)
    return cosp.reshape(b, s, LANES), sinp.reshape(b, s, LANES)


def _log_sigmoid(x):
    return jnp.minimum(x, 0.0) - jnp.log1p(jnp.exp(-jnp.abs(x)))


def _inproj_kernel(x_ref, mod_ref, ng_ref, wqvo_ref, wkt_ref, wg_ref, wgt_ref, bcol_ref, brow_ref,
                   q_ref, kt_ref, v_ref, og_ref, gc_ref, gr_ref):
    m = mod_ref[0]
    xhat = _rms(x_ref[0])
    h = (xhat * (ng_ref[0:1] * (1.0 + m[1:2])) + m[0:1]).astype(BF16)
    nq = A_HEADS * A_DQK
    q = jnp.dot(h, wqvo_ref[:, :nq], preferred_element_type=F32)
    q_ref[0] = (q * (A_DQK ** -0.5)).astype(BF16)
    v = jnp.dot(h, wqvo_ref[:, nq:nq + D_MODEL], preferred_element_type=F32)
    v_ref[0] = v.astype(BF16)
    o = jnp.dot(h, wqvo_ref[:, nq + D_MODEL:], preferred_element_type=F32)
    og_ref[0] = (1.0 / (1.0 + jnp.exp(-o))).astype(BF16)
    nt = (((1,), (1,)), ((), ()))
    kt_ref[0] = lax.dot_general(wkt_ref[...], h, nt, preferred_element_type=F32).astype(BF16)
    gc = jnp.dot(h, wg_ref[...], preferred_element_type=F32) + bcol_ref[...]
    col = lax.broadcasted_iota(jnp.int32, gc.shape, 1)
    gc_ref[0] = jnp.where(col < A_HEADS, gc, _log_sigmoid(gc))
    gr = lax.dot_general(wgt_ref[...], h, nt, preferred_element_type=F32) + brow_ref[...]
    row = lax.broadcasted_iota(jnp.int32, gr.shape, 0)
    gr_ref[0] = jnp.where(row < A_HEADS, gr, _log_sigmoid(gr))


def _inproj(x, modp, ng, wqvo, wkt, wg, wgt, bcol, brow, tm=1024):
    b, s, d = x.shape
    nq = A_HEADS * A_DQK
    ng8 = 2 * A_HEADS
    tok = lambda w: pl.BlockSpec((1, tm, w), lambda i, j: (i, j, 0))
    return pl.pallas_call(
        _inproj_kernel,
        grid=(b, s // tm),
        in_specs=[tok(d),
                  pl.BlockSpec((1, 8, d), lambda i, j: (i, 0, 0)),
                  _const_spec(ng.shape), _const_spec(wqvo.shape), _const_spec(wkt.shape),
                  _const_spec(wg.shape), _const_spec(wgt.shape), _const_spec(bcol.shape),
                  _const_spec(brow.shape)],
        out_specs=[tok(nq),
                   pl.BlockSpec((1, nq, tm), lambda i, j: (i, 0, j)),
                   tok(d), tok(d), tok(LANES),
                   pl.BlockSpec((1, ng8, tm), lambda i, j: (i, 0, j))],
        out_shape=[jax.ShapeDtypeStruct((b, s, nq), BF16),
                   jax.ShapeDtypeStruct((b, nq, s), BF16),
                   jax.ShapeDtypeStruct((b, s, d), BF16),
                   jax.ShapeDtypeStruct((b, s, d), BF16),
                   jax.ShapeDtypeStruct((b, s, LANES), F32),
                   jax.ShapeDtypeStruct((b, ng8, s), F32)],
        compiler_params=_params(("parallel", "parallel")),
        name="mlstm_inproj",
    )(x, modp, ng, wqvo, wkt, wg, wgt, bcol, brow)


def _mlstm_kernel(q_ref, kt_ref, v_ref, og_ref, gc_ref, gr_ref, hn_ref, o_ref, c_scr, n_scr, m_scr, *, L):
    @pl.when(pl.program_id(1) == 0)
    def _():
        c_scr[...] = jnp.zeros_like(c_scr)
        n_scr[...] = jnp.zeros_like(n_scr)
        m_scr[...] = jnp.zeros_like(m_scr)

    r = lax.broadcasted_iota(jnp.int32, (L, L), 0)
    cc = lax.broadcasted_iota(jnp.int32, (L, L), 1)
    causal = cc <= r
    tril = jnp.where(causal, 1.0, 0.0).astype(BF16)
    triu = jnp.where(r <= cc, 1.0, 0.0).astype(BF16)
    gc = gc_ref[0]
    gr = gr_ref[0]
    bcol_all = sum(jnp.dot(tril, t, preferred_element_type=F32) for t in _split3(gc))
    brow_all = sum(jnp.dot(t, triu, preferred_element_type=F32) for t in _split3(gr))
    nt = (((1,), (1,)), ((), ()))

    for hd in range(A_HEADS):
        fq = hd + A_HEADS
        i_row = gr[hd:hd + 1, :]
        b_row = brow_all[fq:fq + 1, :]
        b_col = bcol_all[:, fq:fq + 1]
        m_prev = m_scr[hd][0:1, 0:1]
        q = q_ref[0, :, hd * A_DQK:(hd + 1) * A_DQK]
        kt = kt_ref[0, hd * A_DQK:(hd + 1) * A_DQK, :]
        v = v_ref[0, :, hd * A_DV:(hd + 1) * A_DV]
        c_prev = c_scr[hd]
        n_prev = n_scr[hd][0:1, :]

        dmat = jnp.where(causal, b_col - b_row + i_row, -jnp.inf)
        inter = b_col + m_prev
        m_t = jnp.maximum(inter, jnp.max(dmat, axis=1, keepdims=True))
        w_inter = jnp.exp(inter - m_t)
        p = jnp.exp(dmat - m_t) * jnp.dot(q, kt, preferred_element_type=F32)
        num = (w_inter * jnp.dot(q, c_prev.astype(BF16), preferred_element_type=F32)
               + jnp.dot(p.astype(BF16), v, preferred_element_type=F32))
        den = (w_inter * jnp.sum(q.astype(F32) * n_prev, axis=1, keepdims=True)
               + jnp.sum(p, axis=1, keepdims=True))
        hh = num / jnp.maximum(jnp.abs(den), jnp.exp(-m_t))
        hh = _rms(hh) * hn_ref[hd:hd + 1, :]
        og = og_ref[0, :, hd * A_DV:(hd + 1) * A_DV].astype(F32)
        o_ref[0, :, hd * A_DV:(hd + 1) * A_DV] = (og * hh).astype(o_ref.dtype)

        b_last = b_row[:, L - 1:L]
        g_row = b_last - b_row + i_row
        m_new = jnp.maximum(b_last + m_prev, jnp.max(g_row, axis=1, keepdims=True))
        decay = jnp.exp(b_last + m_prev - m_new)
        wk = jnp.exp(g_row - m_new)
        kw = (kt.astype(F32) * wk).astype(BF16)
        c_scr[hd] = decay * c_prev + jnp.dot(kw, v, preferred_element_type=F32)
        wk8 = jnp.broadcast_to(wk, (8, L))
        n_add = sum(lax.dot_general(t, kt, nt, preferred_element_type=F32) for t in _split3(wk8))
        n_scr[hd] = decay * n_scr[hd] + n_add
        m_scr[hd] = jnp.broadcast_to(m_new, (8, LANES))


def _mlstm(q, kt, v, og, gc, gr, head_norm, L=256):
    b, s, d = v.shape
    nq = A_HEADS * A_DQK
    ng8 = 2 * A_HEADS
    tok = lambda w: pl.BlockSpec((1, L, w), lambda i, j: (i, j, 0))
    return pl.pallas_call(
        functools.partial(_mlstm_kernel, L=L),
        grid=(b, s // L),
        in_specs=[tok(nq),
                  pl.BlockSpec((1, nq, L), lambda i, j: (i, 0, j)),
                  tok(d), tok(d), tok(LANES),
                  pl.BlockSpec((1, ng8, L), lambda i, j: (i, 0, j)),
                  pl.BlockSpec(head_norm.shape, lambda i, j: (0, 0))],
        out_specs=tok(d),
        out_shape=jax.ShapeDtypeStruct((b, s, d), BF16),
        scratch_shapes=[pltpu.VMEM((A_HEADS, A_DQK, A_DV), F32),
                        pltpu.VMEM((A_HEADS, 8, A_DQK), F32),
                        pltpu.VMEM((A_HEADS, 8, LANES), F32)],
        compiler_params=_params(("parallel", "arbitrary")),
        name="mlstm_chunkwise",
    )(q, kt, v, og, gc, gr, head_norm)


def _post_kernel(a_ref, x_ref, mod_ref, ng_ref, wp_ref, w1_ref, w2_ref, o_ref, *, ff_chunk):
    m = mod_ref[0]
    y = jnp.dot(a_ref[0], wp_ref[...], preferred_element_type=F32)
    x1 = x_ref[0] + (m[2:3] * ng_ref[1:2]) * _rms(y)
    h = (_rms(x1) * (ng_ref[2:3] * (1.0 + m[4:5])) + m[3:4]).astype(BF16)
    acc = jnp.zeros_like(x1)
    for c in range(D_FF // ff_chunk):
        u = jnp.dot(h, w1_ref[:, c * ff_chunk:(c + 1) * ff_chunk], preferred_element_type=F32)
        u = jnp.square(jnp.maximum(u, 0.0)).astype(BF16)
        acc = acc + jnp.dot(u, w2_ref[c * ff_chunk:(c + 1) * ff_chunk, :], preferred_element_type=F32)
    o_ref[0] = x1 + (m[5:6] * ng_ref[3:4]) * _rms(acc)


def _post(a, x, modp, ng, wp, w1, w2, tm=1024, ff_chunk=1024):
    b, s, d = x.shape
    tok = pl.BlockSpec((1, tm, d), lambda i, j: (i, j, 0))
    return pl.pallas_call(
        functools.partial(_post_kernel, ff_chunk=ff_chunk),
        grid=(b, s // tm),
        in_specs=[tok, tok,
                  pl.BlockSpec((1, 8, d), lambda i, j: (i, 0, 0)),
                  _const_spec(ng.shape), _const_spec(wp.shape), _const_spec(w1.shape),
                  _const_spec(w2.shape)],
        out_specs=tok,
        out_shape=jax.ShapeDtypeStruct((b, s, d), F32),
        compiler_params=_params(("parallel", "parallel")),
        name="outproj_mlp",
    )(a, x, modp, ng, wp, w1, w2)


def _rope_block(blk, cosp, sinp):
    return blk * cosp + pltpu.roll(blk, LANES // 2, axis=1) * sinp


def _qkv_kernel(x_ref, mod_ref, kvmod_ref, ng_ref, kvn_ref, wqa_ref, qn_ref, wqb_ref, wdkv_ref,
                kln_ref, wukv_ref, cos_ref, sin_ref, q_ref, k_ref, v_ref):
    m = mod_ref[0]
    km = kvmod_ref[0]
    xhat = _rms(x_ref[0])
    cosp = cos_ref[0]
    sinp = sin_ref[0]
    scale = (B_NOPE + B_ROPE) ** -0.5 * LOG2E

    h = (xhat * (ng_ref[0:1] * (1.0 + m[1:2])) + m[0:1]).astype(BF16)
    qa = jnp.dot(h, wqa_ref[...], preferred_element_type=F32)
    qn = (_rms(qa) * qn_ref[...]).astype(BF16)
    q = jnp.dot(qn, wqb_ref[...], preferred_element_type=F32) * scale
    for hd in range(B_HEADS):
        lo = hd * QK_PAD
        q_ref[0, :, lo:lo + B_NOPE] = q[:, lo:lo + B_NOPE].astype(BF16)
        q_ref[0, :, lo + B_NOPE:lo + QK_PAD] = _rope_block(
            q[:, lo + B_NOPE:lo + QK_PAD], cosp, sinp).astype(BF16)

    hs = (xhat * (kvn_ref[...] * (1.0 + km[1:2])) + km[0:1]).astype(BF16)
    dkv = jnp.dot(hs, wdkv_ref[...], preferred_element_type=F32)
    kr = _rope_block(dkv[:, B_KVLORA:], cosp, sinp).astype(BF16)
    ckv = (_rms(dkv[:, :B_KVLORA]) * kln_ref[...]).astype(BF16)
    kv = jnp.dot(ckv, wukv_ref[...], preferred_element_type=F32)
    for hd in range(B_HEADS):
        lo = hd * QK_PAD
        k_ref[0, :, lo:lo + B_NOPE] = kv[:, hd * B_NOPE:(hd + 1) * B_NOPE].astype(BF16)
        k_ref[0, :, lo + B_NOPE:lo + QK_PAD] = kr
    v_ref[0] = kv[:, B_HEADS * B_NOPE:].astype(BF16)


def _qkv(x, modp, kvmodp, ng, kvn, wqa, qnorm, wqb, wdkv, kln, wukv, cosp, sinp, tm=1024):
    b, s, d = x.shape
    tok = lambda w: pl.BlockSpec((1, tm, w), lambda i, j: (i, j, 0))
    modspec = pl.BlockSpec((1, 8, d), lambda i, j: (i, 0, 0))
    hq = B_HEADS * QK_PAD
    return pl.pallas_call(
        _qkv_kernel,
        grid=(b, s // tm),
        in_specs=[tok(d), modspec, modspec,
                  _const_spec(ng.shape), _const_spec(kvn.shape), _const_spec(wqa.shape),
                  _const_spec(qnorm.shape), _const_spec(wqb.shape), _const_spec(wdkv.shape),
                  _const_spec(kln.shape), _const_spec(wukv.shape),
                  tok(LANES), tok(LANES)],
        out_specs=[tok(hq), tok(hq), tok(d)],
        out_shape=[jax.ShapeDtypeStruct((b, s, hq), BF16),
                   jax.ShapeDtypeStruct((b, s, hq), BF16),
                   jax.ShapeDtypeStruct((b, s, d), BF16)],
        compiler_params=_params(("parallel", "parallel")),
        name="mla_qkv",
    )(x, modp, kvmodp, ng, kvn, wqa, qnorm, wqb, wdkv, kln, wukv, cosp, sinp)


def _attn_kernel(q_ref, qnext_ref, k_ref, v_ref, o_ref, vaug_scr, s_scr, smax_scr, m_scr, acc_scr, *, tk):
    i = pl.program_id(2)
    s_len = v_ref.shape[1]
    tq = 2 * tk
    rows = 1024
    nt = (((1,), (1,)), ((), ()))
    lower = slice(0, tk)
    upper = slice(tk, tq)
    every = slice(0, tq)

    def scores(qsrc_ref, j, slot):
        off = pl.multiple_of(j * tk, tk)
        s = lax.dot_general(qsrc_ref[0], k_ref[0, pl.ds(off, tk), :], nt, preferred_element_type=F32)
        s_scr[slot] = s
        smax_scr[slot] = jnp.broadcast_to(jnp.max(s, axis=1, keepdims=True), (tq, LANES))

    @pl.when(i == 0)
    def _():
        ones_blk = jnp.ones((rows, LANES), BF16)

        def fill(r, carry):
            off = pl.multiple_of(r * rows, rows)
            vaug_scr[pl.ds(off, rows), 0:B_VDIM] = v_ref[0, pl.ds(off, rows), :]
            vaug_scr[pl.ds(off, rows), B_VDIM:] = ones_blk
            return carry
        lax.fori_loop(0, s_len // rows, fill, 0)
        scores(q_ref, 0, 0)
        scores(q_ref, 1, 1)

    m_scr[...] = jnp.full(m_scr.shape, -jnp.inf, F32)
    acc_scr[...] = jnp.zeros(acc_scr.shape, F32)

    def update(j, slot, rows_, diagonal):
        off = pl.multiple_of(j * tk, tk)
        s = s_scr[slot, rows_, :]
        if diagonal:
            qc = lax.broadcasted_iota(jnp.int32, s.shape, 0) // CHUNK
            kc = lax.broadcasted_iota(jnp.int32, s.shape, 1) // CHUNK
            s = jnp.where(kc <= qc, s, -jnp.inf)
            smax = jnp.broadcast_to(jnp.max(s, axis=1, keepdims=True), (s.shape[0], LANES))
        else:
            smax = smax_scr[slot, rows_, :]
        m = m_scr[rows_, :]
        m_new = jnp.maximum(m, smax)
        alpha = jnp.exp2(m - m_new)
        p = jnp.exp2(s - jnp.tile(m_new, (1, tk // LANES))).astype(BF16)
        pv = jnp.dot(p, vaug_scr[pl.ds(off, tk), :], preferred_element_type=F32)
        acc_scr[rows_, :] = jnp.tile(alpha, (1, 2 * B_VDIM // LANES)) * acc_scr[rows_, :] + pv
        m_scr[rows_, :] = m_new

    def pair(j):
        update(j, 0, every, False)
        scores(q_ref, j + 2, 0)
        update(j + 1, 1, every, False)
        scores(q_ref, j + 3, 1)

    def two_pairs(t, carry):
        pair(4 * t)
        pair(4 * t + 2)
        return carry
    lax.fori_loop(0, i // 2, two_pairs, 0)

    @pl.when(i % 2 == 1)
    def _():
        pair(2 * i - 2)

    update(2 * i, 0, lower, True)
    update(2 * i, 0, upper, False)
    scores(qnext_ref, 0, 0)
    update(2 * i + 1, 1, upper, True)
    scores(qnext_ref, 1, 1)
    o_ref[0] = (acc_scr[:, :B_VDIM] / acc_scr[:, B_VDIM:]).astype(o_ref.dtype)


def _attention(q, k, v, tk=512):
    b, s, _ = q.shape
    tq = 2 * tk
    nq = s // tq
    return pl.pallas_call(
        functools.partial(_attn_kernel, tk=tk),
        grid=(b, B_HEADS, nq),
        in_specs=[pl.BlockSpec((1, tq, QK_PAD), lambda bi, h, i: (bi, i, h)),
                  pl.BlockSpec((1, tq, QK_PAD), lambda bi, h, i: (bi, jnp.minimum(i + 1, nq - 1), h)),
                  pl.BlockSpec((1, s, QK_PAD), lambda bi, h, i: (bi, 0, h)),
                  pl.BlockSpec((1, s, B_VDIM), lambda bi, h, i: (bi, 0, h))],
        out_specs=pl.BlockSpec((1, tq, B_VDIM), lambda bi, h, i: (bi, i, h)),
        out_shape=jax.ShapeDtypeStruct((b, s, B_HEADS * B_VDIM), BF16),
        scratch_shapes=[pltpu.VMEM((s, 2 * B_VDIM), BF16),
                        pltpu.VMEM((2, tq, tk), F32),
                        pltpu.VMEM((2, tq, LANES), F32),
                        pltpu.VMEM((tq, LANES), F32),
                        pltpu.VMEM((tq, 2 * B_VDIM), F32)],
        compiler_params=_params(("parallel", "parallel", "arbitrary")),
        name="mla_attention",
    )(q, q, k, v)


def _pack_mod(vec6, b):
    m = vec6[:b].reshape(b, 6, D_MODEL)
    return jnp.pad(m, ((0, 0), (0, 2), (0, 0)))


def _rope_cols(w, n_lead):
    half = B_ROPE // 2
    z = jnp.zeros((w.shape[0], half), w.dtype)
    return jnp.concatenate([w[:, :n_lead], w[:, n_lead:n_lead + half], z, w[:, n_lead + half:], z], axis=1)


def kernel(x, c, positions, mod_w, mod_b, norm_g, ffn_w1, ffn_w2, a_w_in, a_b_gates, a_head_norm,
           a_w_out, b_w_qa, b_q_norm, b_w_qb, b_w_o, kv_mod_w, kv_mod_b, kv_norm, w_dkv,
           kv_lora_norm, w_ukv):
    b, s, d = x.shape

    c8 = jnp.pad(c, ((0, 8 - b), (0, 0)))
    mods = _mod_vectors(c8, mod_w, mod_b[:, None, :])
    kvmod = _mod_vectors(c8, kv_mod_w[None], kv_mod_b[None, None, :])[0]
    modp0 = _pack_mod(mods[0], b)
    modp1 = _pack_mod(mods[1], b)
    kvmodp = jnp.pad(kvmod[:b].reshape(b, 2, d), ((0, 0), (0, 6), (0, 0)))

    w_in = a_w_in[0]
    nq = A_HEADS * A_DQK
    wq, wk, wv, wo, wgate = (w_in[:, :nq], w_in[:, nq:2 * nq], w_in[:, 2 * nq:2 * nq + d],
                             w_in[:, 2 * nq + d:2 * nq + 2 * d], w_in[:, 2 * nq + 2 * d:])
    wqvo = jnp.concatenate([wq, wv, wo], axis=1).astype(BF16)
    wkt = wk.T.astype(BF16)
    ng8 = 2 * A_HEADS
    wg = jnp.pad(wgate, ((0, 0), (0, LANES - ng8))).astype(BF16)
    wgt = wgate.T.astype(BF16)
    bcol = jnp.pad(a_b_gates[0], (0, LANES - ng8)).reshape(1, LANES)
    brow = a_b_gates[0].reshape(ng8, 1)
    q, kt, v, og, gc, gr = _inproj(x, modp0, norm_g[0], wqvo, wkt, wg, wgt, bcol, brow)
    a = _mlstm(q, kt, v, og, gc, gr, a_head_norm[0])
    x = _post(a, x, modp0, norm_g[0], a_w_out[0].astype(BF16), ffn_w1[0].astype(BF16),
              ffn_w2[0].astype(BF16))

    cosp, sinp = _rope_tables(positions)
    wqb = b_w_qb[0].reshape(B_QLORA * B_HEADS, B_NOPE + B_ROPE)
    wqb = _rope_cols(wqb, B_NOPE).reshape(B_QLORA, B_HEADS * QK_PAD).astype(BF16)
    wdkv = _rope_cols(w_dkv, B_KVLORA).astype(BF16)
    wukv = w_ukv.reshape(B_KVLORA, B_HEADS, B_NOPE + B_VDIM)
    wukv = jnp.concatenate([wukv[:, :, :B_NOPE].reshape(B_KVLORA, -1),
                            wukv[:, :, B_NOPE:].reshape(B_KVLORA, -1)], axis=1).astype(BF16)
    qh, kh, vh = _qkv(x, modp1, kvmodp, norm_g[1], kv_norm.reshape(1, d), b_w_qa[0].astype(BF16),
                      b_q_norm[0].reshape(1, -1), wqb, wdkv, kv_lora_norm.reshape(1, -1), wukv,
                      cosp, sinp)
    a = _attention(qh, kh, vh)
    x = _post(a, x, modp1, norm_g[1], b_w_o[0].astype(BF16), ffn_w1[1].astype(BF16),
              ffn_w2[1].astype(BF16))
    return x
```
